```python
import jax, jax.numpy as jnp
from jax import lax
import numpy as np

D_MODEL = 1024
BATCH = 4
SEQ = 8192
DEPTH = 2

GROUP_WIDTH = D_MODEL // 4
MIX_WIDTH = 4 * GROUP_WIDTH
FOX_HEADS = 4
FOX_HEAD_DIM = GROUP_WIDTH // FOX_HEADS
Q_BLOCK = 128
GLA_HEADS = 4
GLA_DV = GROUP_WIDTH // GLA_HEADS
GLA_DK = GLA_DV // 2
GLA_GATE_RANK = 16
GLA_GATE_TAU = 16.0
GLA_CHUNK = 64
MLA_HEADS = 4
MLA_NOPE_DIM = 64
MLA_ROPE_DIM = 32
MLA_V_DIM = GROUP_WIDTH // MLA_HEADS
MLA_Q_LORA = 256
MLA_KV_LORA = 128
ROPE_THETA = 10000.0
SSM_D_INNER = GROUP_WIDTH
SSM_HEAD_DIM = 64
SSM_HEADS = SSM_D_INNER // SSM_HEAD_DIM
SSM_GROUPS = 2
SSM_STATE = 128
SSM_CONV = 4
SSM_CHUNK = 128
SSM_CONV_DIM = SSM_D_INNER + 2 * SSM_GROUPS * SSM_STATE
FFN_HIDDEN = -(-8 * D_MODEL // (3 * 256)) * 256
EPS = 1e-6

IN_SPLITS = (
    FOX_HEADS * FOX_HEAD_DIM, FOX_HEADS * FOX_HEAD_DIM, FOX_HEADS * FOX_HEAD_DIM, FOX_HEADS,
    GLA_HEADS * GLA_DK, GLA_HEADS * GLA_DK, GLA_HEADS * GLA_DV, GLA_HEADS * GLA_DV, GLA_GATE_RANK,
    MLA_Q_LORA, MLA_KV_LORA, MLA_ROPE_DIM,
    SSM_D_INNER, SSM_CONV_DIM, SSM_HEADS,
)
IN_COLS = sum(IN_SPLITS)

kernel_name = "hybrid_fox_gla_mla_ssd_block"


def split_cols(t, sizes):
    out, off = [], 0
    for s in sizes:
        out.append(t[..., off:off + s])
        off += s
    return out


def rmsnorm(x, g):
    xf = x.astype(jnp.float32)
    y = xf * lax.rsqrt(jnp.mean(xf * xf, axis=-1, keepdims=True) + EPS)
    return (y * g.astype(jnp.float32)).astype(x.dtype)


def to_heads(t, h):
    b, s, _ = t.shape
    return t.reshape(b, s, h, -1).transpose(0, 2, 1, 3)


def from_heads(t):
    b, h, s, d = t.shape
    return t.transpose(0, 2, 1, 3).reshape(b, s, h * d)


def rope(t, positions):
    half = t.shape[-1] // 2
    inv = ROPE_THETA ** (-jnp.arange(half, dtype=jnp.float32) / half)
    ang = positions.astype(jnp.float32)[..., None] * inv
    ang = ang.reshape(ang.shape[:2] + (1,) * (t.ndim - 3) + (half,))
    cos, sin = jnp.cos(ang), jnp.sin(ang)
    t1, t2 = t[..., :half].astype(jnp.float32), t[..., half:].astype(jnp.float32)
    return jnp.concatenate([t1 * cos - t2 * sin, t1 * sin + t2 * cos], axis=-1).astype(t.dtype)


def causal_block_attention(q, k, v, logf_cum=None):
    b, h, s, dk = q.shape
    dv = v.shape[-1]
    nb = s // Q_BLOCK
    scale = dk ** -0.5
    qb = q.reshape(b, h, nb, Q_BLOCK, dk).transpose(2, 0, 1, 3, 4)
    idx = jnp.arange(nb)
    key_pos = jnp.arange(s)
    if logf_cum is None:
        xs = (idx, qb)
    else:
        xs = (idx, qb, logf_cum.reshape(b, h, nb, Q_BLOCK).transpose(2, 0, 1, 3))

    def one(args):
        i, q_i = args[0], args[1]
        sc = jnp.einsum('bhqd,bhkd->bhqk', q_i, k).astype(jnp.float32) * scale
        if logf_cum is not None:
            sc = sc + args[2][..., :, None] - logf_cum[..., None, :]
        qpos = i * Q_BLOCK + jnp.arange(Q_BLOCK)
        mask = key_pos[None, :] <= qpos[:, None]
        p = jax.nn.softmax(jnp.where(mask, sc, -jnp.inf), axis=-1)
        return jnp.einsum('bhqk,bhkd->bhqd', p.astype(v.dtype), v)

    out = lax.map(one, xs)
    return out.transpose(1, 2, 0, 3, 4).reshape(b, h, s, dv)


def gla_chunked(q, k, v, g):
    b, h, s, dk = q.shape
    dv = v.shape[-1]
    L = GLA_CHUNK
    n = s // L

    def chunks(t):
        return t.reshape(b, h, n, L, t.shape[-1]).transpose(2, 0, 1, 3, 4)

    causal = jnp.tril(jnp.ones((L, L), dtype=bool))

    def step(state, inp):
        q_c, k_c, v_c, g_c = inp
        G = jnp.cumsum(g_c, axis=-2)
        o_inter = jnp.einsum('bhld,bhdv->bhlv', q_c * jnp.exp(G), state)
        diff = G[:, :, :, None, :] - G[:, :, None, :, :]
        decay = jnp.exp(jnp.where(causal[:, :, None], diff, -jnp.inf))
        scores = jnp.einsum('bhid,bhjd,bhijd->bhij', q_c, k_c, decay)
        o_intra = jnp.einsum('bhij,bhjv->bhiv', scores, v_c)
        G_last = G[:, :, -1:, :]
        k_dec = k_c * jnp.exp(G_last - G)
        new_state = jnp.exp(G_last[:, :, 0, :])[..., None] * state + jnp.einsum('bhld,bhlv->bhdv', k_dec, v_c)
        return new_state, (o_inter + o_intra).astype(v.dtype)

    state0 = jnp.zeros((b, h, dk, dv), jnp.float32)
    _, o = lax.scan(step, state0, (chunks(q), chunks(k), chunks(v), chunks(g)))
    return o.transpose(1, 2, 0, 3, 4).reshape(b, h, s, dv)


def segsum(a):
    T = a.shape[-1]
    cs = jnp.cumsum(a, axis=-1)
    diff = cs[..., :, None] - cs[..., None, :]
    return jnp.where(jnp.tril(jnp.ones((T, T), dtype=bool)), diff, -jnp.inf)


def ssd_chunked(X, a, Bh, Ch):
    b, s, h, p = X.shape
    n = Bh.shape[-1]
    L = SSM_CHUNK
    c = s // L
    X = X.reshape(b, c, L, h, p)
    Bh = Bh.reshape(b, c, L, h, n)
    Ch = Ch.reshape(b, c, L, h, n)
    a = a.reshape(b, c, L, h).transpose(0, 3, 1, 2)
    a_cs = jnp.cumsum(a, axis=-1)
    Lmat = jnp.exp(segsum(a))
    scores = jnp.einsum('bclhn,bcshn->bhcls', Ch, Bh) * Lmat
    y_diag = jnp.einsum('bhcls,bcshp->bclhp', scores, X)
    decay_states = jnp.exp(a_cs[..., -1:] - a_cs)
    states = jnp.einsum('bclhn,bhcl,bclhp->bchpn', Bh, decay_states, X)
    states = jnp.concatenate([jnp.zeros_like(states[:, :1]), states], axis=1)
    chunk_decay = jnp.exp(segsum(jnp.pad(a_cs[..., -1], ((0, 0), (0, 0), (1, 0)))))
    states = jnp.einsum('bhzc,bchpn->bzhpn', chunk_decay, states)[:, :-1]
    y_off = jnp.einsum('bclhn,bchpn,bhcl->bclhp', Ch, states, jnp.exp(a_cs))
    return (y_diag + y_off).reshape(b, s, h, p)


def causal_depthwise_conv(x, w, bias):
    K, C = w.shape
    y = lax.conv_general_dilated(x, w[:, None, :], window_strides=(1,), padding=[(K - 1, 0)],
                                 dimension_numbers=('NWC', 'WIO', 'NWC'), feature_group_count=C)
    return y + bias


def fox_mixer(q, k, v, f_logit, f_bias):
    logf = jax.nn.log_sigmoid(f_logit.astype(jnp.float32) + f_bias.astype(jnp.float32))
    F = jnp.cumsum(logf, axis=1).transpose(0, 2, 1)
    o = causal_block_attention(to_heads(q, FOX_HEADS), to_heads(k, FOX_HEADS), to_heads(v, FOX_HEADS), F)
    return from_heads(o)


def gla_mixer(q, k, v, r, gate_lr, w2, b2, out_norm):
    b, s, _ = q.shape
    g = jax.nn.log_sigmoid((gate_lr @ w2 + b2).astype(jnp.float32)) / GLA_GATE_TAU
    o = gla_chunked(to_heads(q, GLA_HEADS) * (GLA_DK ** -0.5), to_heads(k, GLA_HEADS),
                    to_heads(v, GLA_HEADS), to_heads(g, GLA_HEADS))
    o = rmsnorm(o.transpose(0, 2, 1, 3), out_norm) * jax.nn.silu(r.reshape(b, s, GLA_HEADS, GLA_DV))
    return o.reshape(b, s, GLA_HEADS * GLA_DV).astype(q.dtype)


def mla_mixer(c_q, c_kv, k_rope, positions, q_norm, w_uq, kv_norm, w_ukv):
    b, s, _ = c_q.shape
    q = (rmsnorm(c_q, q_norm) @ w_uq).reshape(b, s, MLA_HEADS, MLA_NOPE_DIM + MLA_ROPE_DIM)
    q = jnp.concatenate([q[..., :MLA_NOPE_DIM], rope(q[..., MLA_NOPE_DIM:], positions)], axis=-1)
    kv = (rmsnorm(c_kv, kv_norm) @ w_ukv).reshape(b, s, MLA_HEADS, MLA_NOPE_DIM + MLA_V_DIM)
    kr = jnp.broadcast_to(rope(k_rope, positions)[:, :, None, :], (b, s, MLA_HEADS, MLA_ROPE_DIM))
    k = jnp.concatenate([kv[..., :MLA_NOPE_DIM], kr.astype(kv.dtype)], axis=-1)
    v = kv[..., MLA_NOPE_DIM:]
    o = causal_block_attention(q.transpose(0, 2, 1, 3), k.transpose(0, 2, 1, 3), v.transpose(0, 2, 1, 3))
    return from_heads(o)


def mamba2_mixer(z, xbc, dt_raw, conv_w, conv_b, dt_bias, A_log, D_skip, norm_w):
    b, s, _ = z.shape
    xbc = jax.nn.silu(causal_depthwise_conv(xbc, conv_w, conv_b))
    xs, Bm, Cm = split_cols(xbc, (SSM_D_INNER, SSM_GROUPS * SSM_STATE, SSM_GROUPS * SSM_STATE))
    xs = xs.reshape(b, s, SSM_HEADS, SSM_HEAD_DIM)
    rep = SSM_HEADS // SSM_GROUPS
    Bh = jnp.repeat(Bm.reshape(b, s, SSM_GROUPS, SSM_STATE), rep, axis=2)
    Ch = jnp.repeat(Cm.reshape(b, s, SSM_GROUPS, SSM_STATE), rep, axis=2)
    dt = jax.nn.softplus(dt_raw.astype(jnp.float32) + dt_bias.astype(jnp.float32))
    A = -jnp.exp(A_log.astype(jnp.float32))
    y = ssd_chunked(xs * dt[..., None], A * dt, Bh, Ch)
    y = y + D_skip.astype(jnp.float32)[:, None] * xs
    y = (y.reshape(b, s, SSM_D_INNER) * jax.nn.silu(z)).reshape(b, s, SSM_GROUPS, SSM_D_INNER // SSM_GROUPS)
    y = rmsnorm(y, norm_w.reshape(SSM_GROUPS, -1))
    return y.reshape(b, s, SSM_D_INNER).astype(z.dtype)


def setup_inputs(seed: int = 0) -> dict:
    key = jax.random.key(seed)
    ks = jax.random.split(key, 32)
    f32 = jnp.float32

    def nrm(k, shape, scale):
        return jax.random.normal(k, shape, f32) * scale

    def gain(k, shape):
        return 1.0 + 0.02 * jax.random.normal(k, shape, f32)

    x = jax.random.normal(ks[0], (BATCH, SEQ, D_MODEL), f32)
    offset = jax.random.randint(ks[1], (BATCH, 1), 0, 4096, dtype=jnp.int32)
    positions = offset + jnp.arange(SEQ, dtype=jnp.int32)[None, :]
    dt0 = jnp.exp(jax.random.uniform(ks[14], (DEPTH, SSM_HEADS), f32, np.log(1e-3), np.log(1e-1)))
    return {
        "x": x,
        "positions": positions,
        "norm1": gain(ks[2], (DEPTH, D_MODEL)),
        "w_in": nrm(ks[3], (DEPTH, D_MODEL, IN_COLS), D_MODEL ** -0.5),
        "fox_f_bias": jax.random.uniform(ks[4], (DEPTH, FOX_HEADS), f32, 1.0, 5.0),
        "gla_gate_w2": nrm(ks[5], (DEPTH, GLA_GATE_RANK, GLA_HEADS * GLA_DK), GLA_GATE_RANK ** -0.5),
        "gla_gate_b": nrm(ks[6], (DEPTH, GLA_HEADS * GLA_DK), 0.1),
        "gla_out_norm": gain(ks[7], (DEPTH, GLA_DV)),
        "mla_q_norm": gain(ks[8], (DEPTH, MLA_Q_LORA)),
        "mla_w_uq": nrm(ks[9], (DEPTH, MLA_Q_LORA, MLA_HEADS * (MLA_NOPE_DIM + MLA_ROPE_DIM)), MLA_Q_LORA ** -0.5),
        "mla_kv_norm": gain(ks[10], (DEPTH, MLA_KV_LORA)),
        "mla_w_ukv": nrm(ks[11], (DEPTH, MLA_KV_LORA, MLA_HEADS * (MLA_NOPE_DIM + MLA_V_DIM)), MLA_KV_LORA ** -0.5),
        "ssm_conv_w": nrm(ks[12], (DEPTH, SSM_CONV, SSM_CONV_DIM), SSM_CONV ** -0.5),
        "ssm_conv_b": nrm(ks[13], (DEPTH, SSM_CONV_DIM), 0.02),
        "ssm_dt_bias": dt0 + jnp.log(-jnp.expm1(-dt0)),
        "ssm_A_log": jnp.log(jax.random.uniform(ks[15], (DEPTH, SSM_HEADS), f32, 1.0, 16.0)),
        "ssm_D": gain(ks[16], (DEPTH, SSM_HEADS)),
        "ssm_norm": gain(ks[17], (DEPTH, SSM_D_INNER)),
        "w_out": nrm(ks[18], (DEPTH, MIX_WIDTH, D_MODEL), MIX_WIDTH ** -0.5),
        "norm2": gain(ks[19], (DEPTH, D_MODEL)),
        "w_gate": nrm(ks[20], (DEPTH, D_MODEL, FFN_HIDDEN), D_MODEL ** -0.5),
        "w_up": nrm(ks[21], (DEPTH, D_MODEL, FFN_HIDDEN), D_MODEL ** -0.5),
        "w_down": nrm(ks[22], (DEPTH, FFN_HIDDEN, D_MODEL), FFN_HIDDEN ** -0.5),
        "final_norm": gain(ks[23], (D_MODEL,)),
    }


def reference(x, positions, norm1, w_in, fox_f_bias, gla_gate_w2, gla_gate_b, gla_out_norm,
              mla_q_norm, mla_w_uq, mla_kv_norm, mla_w_ukv, ssm_conv_w, ssm_conv_b, ssm_dt_bias,
              ssm_A_log, ssm_D, ssm_norm, w_out, norm2, w_gate, w_up, w_down, final_norm):
    for l in range(DEPTH):
        h = rmsnorm(x, norm1[l])
        (fq, fk, fv, ff, gq, gk, gv, gr, gg, mcq, mckv, mkr, sz, sxbc, sdt) = split_cols(h @ w_in[l], IN_SPLITS)
        y_a = fox_mixer(fq, fk, fv, ff, fox_f_bias[l])
        y_b = gla_mixer(gq, gk, gv, gr, gg, gla_gate_w2[l], gla_gate_b[l], gla_out_norm[l])
        y_c = mla_mixer(mcq, mckv, mkr, positions, mla_q_norm[l], mla_w_uq[l], mla_kv_norm[l], mla_w_ukv[l])
        y_d = mamba2_mixer(sz, sxbc, sdt, ssm_conv_w[l], ssm_conv_b[l], ssm_dt_bias[l], ssm_A_log[l],
                           ssm_D[l], ssm_norm[l])
        mix = jnp.concatenate([y_a, y_b, y_c, y_d], axis=-1)
        x = x + (mix @ w_out[l]).astype(x.dtype)
        h2 = rmsnorm(x, norm2[l])
        x = x + ((jax.nn.silu(h2 @ w_gate[l]) * (h2 @ w_up[l])) @ w_down[l]).astype(x.dtype)
    return rmsnorm(x, final_norm)
```

```python
import functools
import math

import numpy as np
import jax
import jax.numpy as jnp
from jax import lax
from jax.experimental import pallas as pl
from jax.experimental.pallas import tpu as pltpu

F32 = jnp.float32
BF16 = jnp.bfloat16

D_MODEL = 1024
GROUP_WIDTH = 256
HEADS = 4
HEAD_DIM = 64
GLA_DK = 32
GLA_GATE_RANK = 16
GLA_GATE_TAU = 16.0
GLA_CHUNK = 64
GLA_SUB = 16
MLA_ROPE = 32
MLA_Q_LORA = 256
MLA_KV_LORA = 128
ROPE_THETA = 10000.0
SSM_STATE = 128
SSM_CONV = 4
SSM_CHUNK = 128
SSM_CONV_DIM = 768
FFN_HIDDEN = 2816
EPS = 1e-6
NEG = -1e30

LANES = 128
QK_PAD = 128
VMEM_LIMIT = 56 * 1024 * 1024

SEG_FOX = (0, 768)
SEG_GLA = (768, 1536)
SEG_MLA = (1536, 1920)
SEG_SSM = (1920, 2944)
SEG_SMALL = (2944, 3072)
IN_PAD = 3072
SMALL_FOX_F = 0
SMALL_DT = 4
SMALL_GATE = 16
SMALL_KROPE = 64

TILE_ROWS = 512
ATT_TQ = 512
ATT_TK = 512
SCAN_ROWS = 256
FFN_TH = 1408


def _cparams(sem):
    return pltpu.CompilerParams(dimension_semantics=sem, vmem_limit_bytes=VMEM_LIMIT)


def _rms(x, g):
    ms = jnp.mean(x * x, axis=-1, keepdims=True)
    return x * lax.rsqrt(ms + EPS) * g


def _log_sigmoid(x):
    return jnp.minimum(x, 0.0) - jnp.log1p(jnp.exp(-jnp.abs(x)))


def _softplus(x):
    return jnp.maximum(x, 0.0) + jnp.log1p(jnp.exp(-jnp.abs(x)))


def _silu(x):
    return x / (1.0 + jnp.exp(-x))


def _split3(x):
    hi = x.astype(BF16)
    r = x - hi.astype(F32)
    mid = r.astype(BF16)
    lo = (r - mid.astype(F32)).astype(BF16)
    return hi, mid, lo


def _dot(a, b):
    return jnp.dot(a, b, preferred_element_type=F32)


def _dot_nt(a, b):
    return lax.dot_general(a, b, (((1,), (1,)), ((), ())), preferred_element_type=F32)


def _dot_tn(a, b):
    return lax.dot_general(a, b, (((0,), (0,)), ((), ())), preferred_element_type=F32)


def _exact_lhs_dot(m01, x):
    hi, mid, lo = _split3(x)
    return _dot(m01, hi) + _dot(m01, mid) + _dot(m01, lo)


def _inproj_body(x_ref, g_ref, w_ref, fox_ref, gla_ref, mla_ref, ssm_ref, small_ref):
    h = _rms(x_ref[...], g_ref[...]).astype(BF16)

    def seg(s):
        return _dot(h, w_ref[:, s[0]:s[1]])

    fox_ref[...] = seg(SEG_FOX).astype(BF16)
    gla_ref[...] = seg(SEG_GLA)
    mla_ref[...] = seg(SEG_MLA)
    ssm_ref[...] = seg(SEG_SSM)
    small_ref[...] = seg(SEG_SMALL)


def _inproj(x2, g, w):
    t = x2.shape[0]
    tm = TILE_ROWS
    widths = [s[1] - s[0] for s in (SEG_FOX, SEG_GLA, SEG_MLA, SEG_SSM, SEG_SMALL)]
    dts = [BF16, F32, F32, F32, F32]
    return pl.pallas_call(
        _inproj_body,
        grid=(t // tm,),
        in_specs=[pl.BlockSpec((tm, D_MODEL), lambda i: (i, 0)),
                  pl.BlockSpec((1, D_MODEL), lambda i: (0, 0)),
                  pl.BlockSpec((D_MODEL, IN_PAD), lambda i: (0, 0))],
        out_specs=[pl.BlockSpec((tm, w_), lambda i: (i, 0)) for w_ in widths],
        out_shape=[jax.ShapeDtypeStruct((t, w_), d_) for w_, d_ in zip(widths, dts)],
        compiler_params=_cparams(("parallel",)),
        name="inproj",
    )(x2, g, w)


def _rope_table_body(pos_ref, invf_ref, cos_ref, sin_ref):
    ang = pos_ref[...].astype(F32) * invf_ref[...]
    cos_ref[...] = jnp.cos(ang)
    sin_ref[...] = jnp.sin(ang)


def _rope_tables(pos_col, invf):
    t = pos_col.shape[0]
    tm = TILE_ROWS
    return pl.pallas_call(
        _rope_table_body,
        grid=(t // tm,),
        in_specs=[pl.BlockSpec((tm, 1), lambda i: (i, 0)),
                  pl.BlockSpec((1, LANES), lambda i: (0, 0))],
        out_specs=[pl.BlockSpec((tm, LANES), lambda i: (i, 0))] * 2,
        out_shape=[jax.ShapeDtypeStruct((t, LANES), F32)] * 2,
        compiler_params=_cparams(("parallel",)),
        name="rope_tables",
    )(pos_col, invf)


def _fox_prep_body(fox_ref, small_ref, fb_ref, tri_ref, qT_ref, kA_ref, vT_ref, carry_ref):
    ts = fox_ref.shape[0]

    @pl.when(pl.program_id(1) == 0)
    def _():
        carry_ref[...] = jnp.zeros_like(carry_ref)

    logf = _log_sigmoid(small_ref[...] + fb_ref[...])
    fcum = _exact_lhs_dot(tri_ref[...], logf) + carry_ref[...]
    carry_ref[...] = fcum[ts - 1:ts, :]
    f_hi, f_mid, f_lo = [p.astype(F32) for p in _split3(fcum)]

    lane = lax.broadcasted_iota(jnp.int32, (ts, LANES), 1)
    low = lane < HEAD_DIM
    for pair in range(2):
        qp = fox_ref[:, pair * LANES:(pair + 1) * LANES].astype(F32) * (HEAD_DIM ** -0.5)
        kp = fox_ref[:, 256 + pair * LANES:256 + (pair + 1) * LANES].astype(F32)
        vp = fox_ref[:, 512 + pair * LANES:512 + (pair + 1) * LANES].astype(F32)
        vT_ref[0, 0, pair * LANES:(pair + 1) * LANES, :] = vp.T.astype(BF16)
        for e in range(2):
            h = 2 * pair + e
            qh = qp if e == 0 else pltpu.roll(qp, HEAD_DIM, 1)
            kh = kp if e == 0 else pltpu.roll(kp, HEAD_DIM, 1)

            def col(src, dst):
                return pltpu.roll(src, (dst - h) % LANES, 1)

            qa = jnp.where(low, qh,
                 jnp.where(lane == 64, col(f_hi, 64),
                 jnp.where(lane == 65, col(f_mid, 65),
                 jnp.where(lane == 66, col(f_lo, 66),
                 jnp.where(lane < 70, 1.0, 0.0)))))
            ka = jnp.where(low, kh,
                 jnp.where(lane < 67, 1.0,
                 jnp.where(lane == 67, -col(f_hi, 67),
                 jnp.where(lane == 68, -col(f_mid, 68),
                 jnp.where(lane == 69, -col(f_lo, 69), 0.0)))))
            qT_ref[0, h] = qa.T.astype(BF16)
            kA_ref[0, h] = ka.astype(BF16)


def _att_out_shapes(b, s, ts):
    return [jax.ShapeDtypeStruct((b, HEADS, QK_PAD, s), BF16),
            jax.ShapeDtypeStruct((b, HEADS, s, QK_PAD), BF16),
            jax.ShapeDtypeStruct((b, s // ts, GROUP_WIDTH, ts), BF16)]


def _att_out_specs(ts):
    return [pl.BlockSpec((1, HEADS, QK_PAD, ts), lambda b, i: (b, 0, 0, i)),
            pl.BlockSpec((1, HEADS, ts, QK_PAD), lambda b, i: (b, 0, i, 0)),
            pl.BlockSpec((1, 1, GROUP_WIDTH, ts), lambda b, i: (b, i, 0, 0))]


def _fox_prep(fox, small, fbias, tri, b, s):
    ts = ATT_TK
    ns = s // ts
    return pl.pallas_call(
        _fox_prep_body,
        grid=(b, ns),
        in_specs=[pl.BlockSpec((ts, 768), lambda bi, i: (bi * ns + i, 0)),
                  pl.BlockSpec((ts, LANES), lambda bi, i: (bi * ns + i, 0)),
                  pl.BlockSpec((1, LANES), lambda bi, i: (0, 0)),
                  pl.BlockSpec((ts, ts), lambda bi, i: (0, 0))],
        out_specs=_att_out_specs(ts),
        out_shape=_att_out_shapes(b, s, ts),
        scratch_shapes=[pltpu.VMEM((1, LANES), F32)],
        compiler_params=_cparams(("parallel", "arbitrary")),
        name="fox_prep",
    )(fox, small, fbias, tri)


def _mla_prep_body(mla_ref, small_ref, cos_ref, sin_ref, qn_ref, kvn_ref, wq_ref, wk_ref, wv_ref,
                   qT_ref, kA_ref, vT_ref):
    ts = mla_ref.shape[0]
    lane = lax.broadcasted_iota(jnp.int32, (ts, LANES), 1)
    cosv = cos_ref[...]
    sinv = sin_ref[...]
    half = MLA_ROPE // 2
    sin_a = jnp.where((lane >= 64) & (lane < 64 + half), -sinv, 0.0)
    sin_b = jnp.where((lane >= 64 + half) & (lane < 64 + MLA_ROPE), sinv, 0.0)

    def rope(x):
        return (x * cosv + pltpu.roll(x, LANES - half, 1) * sin_a + pltpu.roll(x, half, 1) * sin_b)

    cq = _rms(mla_ref[:, 0:MLA_Q_LORA], qn_ref[...]).astype(BF16)
    ckv = _rms(mla_ref[:, MLA_Q_LORA:MLA_Q_LORA + MLA_KV_LORA], kvn_ref[...]).astype(BF16)
    kr = rope(jnp.where((lane >= 64) & (lane < 64 + MLA_ROPE), small_ref[...], 0.0))
    q_all = _dot(cq, wq_ref[...])
    k_all = _dot(ckv, wk_ref[...])
    v_all = _dot(ckv, wv_ref[...])
    for h in range(HEADS):
        qh = rope(q_all[:, h * LANES:(h + 1) * LANES])
        qT_ref[0, h] = qh.T.astype(BF16)
        kA_ref[0, h] = (k_all[:, h * LANES:(h + 1) * LANES] + kr).astype(BF16)
    for pair in range(2):
        vT_ref[0, 0, pair * LANES:(pair + 1) * LANES, :] = (
            v_all[:, pair * LANES:(pair + 1) * LANES].T.astype(BF16))


def _mla_prep(mla, small, cos_t, sin_t, qn, kvn, wq, wk, wv, b, s):
    ts = ATT_TK
    ns = s // ts
    row = lambda bi, i: (bi * ns + i, 0)
    full = lambda bi, i: (0, 0)
    return pl.pallas_call(
        _mla_prep_body,
        grid=(b, ns),
        in_specs=[pl.BlockSpec((ts, 384), row),
                  pl.BlockSpec((ts, LANES), row),
                  pl.BlockSpec((ts, LANES), row),
                  pl.BlockSpec((ts, LANES), row),
                  pl.BlockSpec((1, MLA_Q_LORA), full),
                  pl.BlockSpec((1, MLA_KV_LORA), full),
                  pl.BlockSpec((MLA_Q_LORA, HEADS * LANES), full),
                  pl.BlockSpec((MLA_KV_LORA, HEADS * LANES), full),
                  pl.BlockSpec((MLA_KV_LORA, GROUP_WIDTH), full)],
        out_specs=_att_out_specs(ts),
        out_shape=_att_out_shapes(b, s, ts),
        compiler_params=_cparams(("parallel", "parallel")),
        name="mla_prep",
    )(mla, small, cos_t, sin_t, qn, kvn, wq, wk, wv)


def _flash_body(qT_ref, k_ref, vT_ref, o_ref, m_ref, l_ref, acc_ref, *, tq, tk, c):
    i = pl.program_id(2)
    qT = qT_ref[0, 0]
    m_ref[...] = jnp.full_like(m_ref, NEG)
    l_ref[...] = jnp.zeros_like(l_ref)
    acc_ref[...] = jnp.zeros_like(acc_ref)
    per_q = tq // tk

    def step(j, diag):
        start = pl.multiple_of(j * tk, tk)
        s = _dot(k_ref[0, 0, pl.ds(start, tk), :], qT)
        if diag is not None:
            kv = lax.broadcasted_iota(jnp.int32, (tk, tq), 0) + diag * tk
            qi = lax.broadcasted_iota(jnp.int32, (tk, tq), 1)
            s = jnp.where(kv <= qi, s, NEG)
        m_prev = m_ref[...]
        m_new = jnp.maximum(m_prev, jnp.max(s, axis=0, keepdims=True))
        alpha = jnp.exp2((m_prev - m_new) * c)
        p = jnp.exp2((s - m_new) * c)
        l_ref[...] = alpha * l_ref[...] + jnp.sum(p, axis=0, keepdims=True)
        acc_ref[...] = alpha * acc_ref[...] + _dot(vT_ref[0, j, 0], p.astype(BF16))
        m_ref[...] = m_new

    def body(j, carry):
        step(j, None)
        return carry

    lax.fori_loop(0, i * per_q, body, 0)
    for d in range(per_q):
        step(i * per_q + d, d)
    o_ref[0, 0] = (acc_ref[...] / l_ref[...]).astype(o_ref.dtype)


def _flash(qT, kA, vT, c):
    b, _, _, s = qT.shape
    tq, tk = ATT_TQ, ATT_TK
    nk = s // tk
    vT5 = vT.reshape(b, nk, HEADS, HEAD_DIM, tk)
    return pl.pallas_call(
        functools.partial(_flash_body, tq=tq, tk=tk, c=c),
        grid=(b, HEADS, s // tq),
        in_specs=[pl.BlockSpec((1, 1, QK_PAD, tq), lambda bi, h, i: (bi, h, 0, i)),
                  pl.BlockSpec((1, 1, s, QK_PAD), lambda bi, h, i: (bi, h, 0, 0)),
                  pl.BlockSpec((1, nk, 1, HEAD_DIM, tk), lambda bi, h, i: (bi, 0, h, 0, 0))],
        out_specs=pl.BlockSpec((1, 1, HEAD_DIM, tq), lambda bi, h, i: (bi, h, 0, i)),
        out_shape=jax.ShapeDtypeStruct((b, HEADS, HEAD_DIM, s), BF16),
        scratch_shapes=[pltpu.VMEM((1, tq), F32), pltpu.VMEM((1, tq), F32),
                        pltpu.VMEM((HEAD_DIM, tq), F32)],
        compiler_params=_cparams(("parallel", "parallel", "arbitrary")),
        name="flash",
    )(qT, kA, vT5)


def _gla_consts(ts):
    idx = np.arange(ts)
    same = (idx[:, None] // GLA_CHUNK) == (idx[None, :] // GLA_CHUNK)
    tri = same & (idx[None, :] <= idx[:, None])
    sub_start = (idx // GLA_SUB) * GLA_SUB
    ref = same & (idx[None, :] < sub_start[:, None])
    mbig = np.concatenate([tri, ref, same], axis=0).astype(np.float32)
    hd = np.arange(HEADS * GLA_DK) // GLA_DK
    hv = np.arange(GROUP_WIDTH) // HEAD_DIM
    ind_dk_dv = (hd[:, None] == hv[None, :]).astype(np.float32)
    ind_mean = (hv[:, None] == hv[None, :]).astype(np.float32) / HEAD_DIM
    return (jnp.asarray(mbig, BF16), jnp.asarray(ind_dk_dv, BF16), jnp.asarray(ind_mean, BF16))


def _pick_heads(stacked, rows):
    lane_h = lax.broadcasted_iota(jnp.int32, (rows, GROUP_WIDTH), 1) // HEAD_DIM
    out = jnp.zeros((rows, GROUP_WIDTH), F32)
    for h in range(HEADS):
        out = jnp.where(lane_h == h, stacked[h * rows:(h + 1) * rows, :], out)
    return out


def _gla_body(gla_ref, small_ref, w2_ref, b2_ref, on_ref, mbig_ref, ind_ref, indm_ref, o_ref, st_ref):
    ts = gla_ref.shape[0]
    L = GLA_CHUNK
    nsub = L // GLA_SUB

    @pl.when(pl.program_id(1) == 0)
    def _():
        st_ref[...] = jnp.zeros_like(st_ref)

    q = gla_ref[:, 0:128] * (GLA_DK ** -0.5)
    k = gla_ref[:, 128:256]
    v = gla_ref[:, 256:512]
    r = gla_ref[:, 512:768]
    x = _dot(small_ref[...].astype(BF16), w2_ref[...]) + b2_ref[...]
    g = _log_sigmoid(x) * (1.0 / GLA_GATE_TAU)
    gg = _exact_lhs_dot(mbig_ref[...], g)
    G = gg[0:ts]
    R = gg[ts:2 * ts]
    GL = gg[2 * ts:3 * ts]
    q_inter = q * jnp.exp(G)
    q_norm = q * jnp.exp(G - R)
    k_dec = k * jnp.exp(GL - G)

    lane_h = lax.broadcasted_iota(jnp.int32, (L, 128), 1) // GLA_DK
    row = lax.broadcasted_iota(jnp.int32, (L, 128), 0)
    sub_pos = row % GLA_SUB
    row_blk = lax.broadcasted_iota(jnp.int32, (L, L * 2), 0) // GLA_SUB
    zeros_v = jnp.zeros((L, GROUP_WIDTH), BF16)

    def stack_heads(a):
        return jnp.concatenate([jnp.where(lane_h == h, a, 0.0) for h in range(HEADS)], axis=0)

    outs = []
    for c in range(ts // L):
        sl = slice(c * L, (c + 1) * L)
        Gc, kc, vc, qc = G[sl], k[sl], v[sl], q[sl]
        vc16 = vc.astype(BF16)

        kts = []
        for I in range(1, nsub):
            r_i = R[c * L + I * GLA_SUB:c * L + I * GLA_SUB + 1, :]
            kt = kc * jnp.exp(jnp.where(row < I * GLA_SUB, r_i - Gc, NEG))
            kts += [kt, jnp.zeros_like(kt)]
        kstack = jnp.concatenate(kts, axis=0).astype(BF16)
        sc = _dot_nt(stack_heads(q_norm[sl]).astype(BF16), kstack)
        a_rows = []
        for h in range(HEADS):
            sh = sc[h * L:(h + 1) * L]
            a_h = jnp.zeros((L, 2 * L), F32)
            for I in range(1, nsub):
                a_h = jnp.where(row_blk == I, sh[:, (I - 1) * 2 * L:I * 2 * L], a_h)
            a_rows.append(a_h)
        a_st = jnp.concatenate(a_rows, axis=0).astype(BF16)
        v_pad = jnp.concatenate([vc16, zeros_v], axis=0)
        st = st_ref[...]
        big = _dot(a_st, v_pad) + _dot_nt(stack_heads(q_inter[sl]).astype(BF16), st.astype(BF16))
        o_c = _pick_heads(big, L)

        ps = []
        for d in range(GLA_SUB):
            if d == 0:
                p = qc * kc
            else:
                arg = jnp.where(sub_pos >= d, Gc - pltpu.roll(Gc, d, 0), NEG)
                p = qc * pltpu.roll(kc, d, 0) * jnp.exp(arg)
            ps.append(p)
        abc = _dot(jnp.concatenate(ps, axis=0).astype(BF16), ind_ref[...])
        for d in range(GLA_SUB):
            vd = vc if d == 0 else pltpu.roll(vc, d, 0)
            o_c = o_c + abc[d * L:(d + 1) * L] * vd
        outs.append(o_c)

        st_ref[...] = (jnp.exp(GL[c * L:c * L + 1, :]) * st
                       + _dot_tn(vc16, k_dec[sl].astype(BF16)))

    o = jnp.concatenate(outs, axis=0)
    o2 = o * o
    hi = o2.astype(BF16)
    lo = (o2 - hi.astype(F32)).astype(BF16)
    ms = _dot(hi, indm_ref[...]) + _dot(lo, indm_ref[...])
    o_ref[...] = (o * lax.rsqrt(ms + EPS) * on_ref[...] * _silu(r)).astype(o_ref.dtype)


def _gla(gla, small, w2p, b2, onorm, b, s):
    ts = SCAN_ROWS
    ns = s // ts
    mbig, ind, indm = _gla_consts(ts)
    row = lambda bi, i: (bi * ns + i, 0)
    full = lambda bi, i: (0, 0)
    return pl.pallas_call(
        _gla_body,
        grid=(b, ns),
        in_specs=[pl.BlockSpec((ts, 768), row),
                  pl.BlockSpec((ts, LANES), row),
                  pl.BlockSpec((LANES, 128), full),
                  pl.BlockSpec((1, 128), full),
                  pl.BlockSpec((1, GROUP_WIDTH), full),
                  pl.BlockSpec((3 * ts, ts), full),
                  pl.BlockSpec((128, GROUP_WIDTH), full),
                  pl.BlockSpec((GROUP_WIDTH, GROUP_WIDTH), full)],
        out_specs=pl.BlockSpec((ts, GROUP_WIDTH), row),
        out_shape=jax.ShapeDtypeStruct((b * s, GROUP_WIDTH), BF16),
        scratch_shapes=[pltpu.VMEM((GROUP_WIDTH, 128), F32)],
        compiler_params=_cparams(("parallel", "arbitrary")),
        name="gla",
    )(gla, small, w2p, b2, onorm, mbig, ind, indm)


def _head_cols(src, lane0, rows, width, nheads):
    lane_h = lax.broadcasted_iota(jnp.int32, (rows, nheads * width), 1) // width
    out = jnp.broadcast_to(src[:, lane0:lane0 + 1], (rows, nheads * width))
    for h in range(1, nheads):
        out = jnp.where(lane_h == h, jnp.broadcast_to(src[:, lane0 + h:lane0 + h + 1],
                                                      (rows, nheads * width)), out)
    return out


def _ssd_body(ssm_ref, small_ref, cw_ref, cb_ref, dtb_ref, alog_ref, dskip_ref, nw_ref, tri_ref,
              o_ref, prev_ref, st_ref):
    ts = ssm_ref.shape[0]
    L = SSM_CHUNK

    @pl.when(pl.program_id(1) == 0)
    def _():
        prev_ref[...] = jnp.zeros_like(prev_ref)
        st_ref[...] = jnp.zeros_like(st_ref)

    z = ssm_ref[:, 0:GROUP_WIDTH]
    xbc = ssm_ref[:, GROUP_WIDTH:GROUP_WIDTH + SSM_CONV_DIM]
    prev8 = prev_ref[...]
    row8 = lax.broadcasted_iota(jnp.int32, (8, SSM_CONV_DIM), 0)
    conv = cb_ref[...] + cw_ref[SSM_CONV - 1:SSM_CONV, :] * xbc
    for sft in range(1, SSM_CONV):
        rolled = pltpu.roll(xbc, sft, 0)
        top = jnp.where(row8 < sft, pltpu.roll(prev8, sft, 0), rolled[0:8])
        shifted = jnp.concatenate([top, rolled[8:]], axis=0)
        conv = conv + cw_ref[SSM_CONV - 1 - sft:SSM_CONV - sft, :] * shifted
    prev_ref[...] = xbc[ts - 8:ts]
    xc = _silu(conv)
    xs = xc[:, 0:GROUP_WIDTH]
    bm = xc[:, GROUP_WIDTH:GROUP_WIDTH + 2 * SSM_STATE]
    cm = xc[:, GROUP_WIDTH + 2 * SSM_STATE:]

    dt = _softplus(small_ref[...] + dtb_ref[...])
    a = -jnp.exp(alog_ref[...]) * dt
    xdt = xs * _head_cols(dt, SMALL_DT, ts, HEAD_DIM, HEADS)

    ii = lax.broadcasted_iota(jnp.int32, (L, L), 0)
    jj = lax.broadcasted_iota(jnp.int32, (L, L), 1)
    lane_lo = jj < HEAD_DIM
    row_lo = ii < HEAD_DIM
    tri = tri_ref[...]
    ys = []
    for c in range(ts // L):
        sl = slice(c * L, (c + 1) * L)
        cs = _exact_lhs_dot(tri, a[sl])
        cs_t = cs.T
        cs_last = cs[L - 1:L, :]
        y_groups = []
        for grp in range(2):
            h0 = 2 * grp
            b_g = bm[sl, grp * SSM_STATE:(grp + 1) * SSM_STATE].astype(BF16)
            c_g = cm[sl, grp * SSM_STATE:(grp + 1) * SSM_STATE].astype(BF16)
            x_g = xdt[sl, grp * LANES:(grp + 1) * LANES]
            cb = _dot_nt(c_g, b_g)
            ws = []
            for h in (h0, h0 + 1):
                colh = cs[:, SMALL_DT + h:SMALL_DT + h + 1]
                rowh = cs_t[SMALL_DT + h:SMALL_DT + h + 1, :]
                ws.append(cb * jnp.exp(jnp.where(jj <= ii, colh - rowh, NEG)))
            yd = _dot(jnp.concatenate(ws, axis=0).astype(BF16), x_g.astype(BF16))
            y_diag = jnp.where(lane_lo, yd[0:L], yd[L:2 * L])
            cs_g = _head_cols(cs, SMALL_DT + h0, L, HEAD_DIM, 2)
            st_g = st_ref[grp * LANES:(grp + 1) * LANES, :]
            y_off = _dot_nt(c_g, st_g.astype(BF16)) * jnp.exp(cs_g)
            y_groups.append(y_diag + y_off)
            last_g = _head_cols(cs_last, SMALL_DT + h0, 1, HEAD_DIM, 2)
            x_dec = x_g * jnp.exp(last_g - cs_g)
            new = _dot_tn(x_dec.astype(BF16), b_g)
            e0 = jnp.exp(cs_last[:, SMALL_DT + h0:SMALL_DT + h0 + 1])
            e1 = jnp.exp(cs_last[:, SMALL_DT + h0 + 1:SMALL_DT + h0 + 2])
            st_ref[grp * LANES:(grp + 1) * LANES, :] = jnp.where(row_lo, e0, e1) * st_g + new
        ys.append(jnp.concatenate(y_groups, axis=1))
    y = jnp.concatenate(ys, axis=0)
    y = (y + dskip_ref[...] * xs) * _silu(z)
    halves = []
    for grp in range(2):
        yg = y[:, grp * LANES:(grp + 1) * LANES]
        halves.append(_rms(yg, nw_ref[:, grp * LANES:(grp + 1) * LANES]))
    o_ref[...] = jnp.concatenate(halves, axis=1).astype(o_ref.dtype)


def _ssd(ssm, small, cw, cb, dtb, alog, dskip, nw, b, s):
    ts = SCAN_ROWS
    ns = s // ts
    tri = jnp.asarray(np.tril(np.ones((SSM_CHUNK, SSM_CHUNK), np.float32)), BF16)
    row = lambda bi, i: (bi * ns + i, 0)
    full = lambda bi, i: (0, 0)
    return pl.pallas_call(
        _ssd_body,
        grid=(b, ns),
        in_specs=[pl.BlockSpec((ts, 1024), row),
                  pl.BlockSpec((ts, LANES), row),
                  pl.BlockSpec((SSM_CONV, SSM_CONV_DIM), full),
                  pl.BlockSpec((1, SSM_CONV_DIM), full),
                  pl.BlockSpec((1, LANES), full),
                  pl.BlockSpec((1, LANES), full),
                  pl.BlockSpec((1, GROUP_WIDTH), full),
                  pl.BlockSpec((1, GROUP_WIDTH), full),
                  pl.BlockSpec((SSM_CHUNK, SSM_CHUNK), full)],
        out_specs=pl.BlockSpec((ts, GROUP_WIDTH), row),
        out_shape=jax.ShapeDtypeStruct((b * s, GROUP_WIDTH), BF16),
        scratch_shapes=[pltpu.VMEM((8, SSM_CONV_DIM), F32), pltpu.VMEM((GROUP_WIDTH, SSM_STATE), F32)],
        compiler_params=_cparams(("parallel", "arbitrary")),
        name="ssd",
    )(ssm, small, cw, cb, dtb, alog, dskip, nw, tri)


def _out_ffn_body(x_ref, yaT_ref, yb_ref, ycT_ref, yd_ref, wo_ref, n2_ref, wg_ref, wu_ref, wd_ref,
                  fn_ref, o_ref, x1_ref, h2_ref, acc_ref, *, final_norm):
    kk = pl.program_id(1)

    @pl.when(kk == 0)
    def _():
        x1 = (x_ref[...]
              + _dot_tn(yaT_ref[0], wo_ref[0]) + _dot(yb_ref[...], wo_ref[1])
              + _dot_tn(ycT_ref[0], wo_ref[2]) + _dot(yd_ref[...], wo_ref[3]))
        x1_ref[...] = x1
        h2_ref[...] = _rms(x1, n2_ref[...]).astype(BF16)

    h2 = h2_ref[...]
    hidden = _silu(_dot(h2, wg_ref[...])) * _dot(h2, wu_ref[...])
    part = _dot(hidden.astype(BF16), wd_ref[...])

    @pl.when(kk == 0)
    def _():
        acc_ref[...] = part

    @pl.when(kk != 0)
    def _():
        acc_ref[...] += part

    @pl.when(kk == pl.num_programs(1) - 1)
    def _():
        out = x1_ref[...] + acc_ref[...]
        if final_norm:
            out = _rms(out, fn_ref[...])
        o_ref[...] = out


def _out_ffn(x2, yaT, yb, ycT, yd, wo, n2, wg, wu, wd, fn, b, s, final_norm):
    tm = TILE_ROWS
    th = FFN_TH
    ns = s // tm
    row = lambda i, k: (i, 0)
    full2 = lambda i, k: (0, 0)
    tr = lambda i, k: (i // ns, 0, i % ns)
    return pl.pallas_call(
        functools.partial(_out_ffn_body, final_norm=final_norm),
        grid=(b * ns, FFN_HIDDEN // th),
        in_specs=[pl.BlockSpec((tm, D_MODEL), row),
                  pl.BlockSpec((1, GROUP_WIDTH, tm), tr),
                  pl.BlockSpec((tm, GROUP_WIDTH), row),
                  pl.BlockSpec((1, GROUP_WIDTH, tm), tr),
                  pl.BlockSpec((tm, GROUP_WIDTH), row),
                  pl.BlockSpec((4, GROUP_WIDTH, D_MODEL), lambda i, k: (0, 0, 0)),
                  pl.BlockSpec((1, D_MODEL), full2),
                  pl.BlockSpec((D_MODEL, th), lambda i, k: (0, k)),
                  pl.BlockSpec((D_MODEL, th), lambda i, k: (0, k)),
                  pl.BlockSpec((th, D_MODEL), lambda i, k: (k, 0)),
                  pl.BlockSpec((1, D_MODEL), full2)],
        out_specs=pl.BlockSpec((tm, D_MODEL), row),
        out_shape=jax.ShapeDtypeStruct((b * s, D_MODEL), F32),
        scratch_shapes=[pltpu.VMEM((tm, D_MODEL), F32), pltpu.VMEM((tm, D_MODEL), BF16),
                        pltpu.VMEM((tm, D_MODEL), F32)],
        compiler_params=_cparams(("parallel", "arbitrary")),
        name="out_ffn",
    )(x2, yaT, yb, ycT, yd, wo, n2, wg, wu, wd, fn)


def _pad_cols(w, n):
    return jnp.pad(w, ((0, 0), (0, n - w.shape[1])))


def _prep_w_in(w):
    z = lambda n: jnp.zeros((w.shape[0], n), w.dtype)
    small = jnp.concatenate([w[:, 768:772], w[:, 2996:3000], z(8), w[:, 1540:1556], z(32),
                             w[:, 1940:1972], z(32)], axis=1)
    out = jnp.concatenate([w[:, 0:768], w[:, 772:1540], w[:, 1556:1940], w[:, 1972:2996], small], axis=1)
    return out.astype(BF16)


def _lane_row(vals, lane0):
    return jnp.zeros((1, LANES), F32).at[0, lane0:lane0 + vals.shape[0]].set(vals.astype(F32))


def _prep_mla_w(w_uq, w_ukv):
    wq = w_uq.reshape(MLA_Q_LORA, HEADS, HEAD_DIM + MLA_ROPE)
    wq = jnp.pad(wq, ((0, 0), (0, 0), (0, LANES - HEAD_DIM - MLA_ROPE))).reshape(MLA_Q_LORA, HEADS * LANES)
    wkv = w_ukv.reshape(MLA_KV_LORA, HEADS, 2 * HEAD_DIM)
    wk = jnp.pad(wkv[:, :, :HEAD_DIM], ((0, 0), (0, 0), (0, LANES - HEAD_DIM))).reshape(MLA_KV_LORA, HEADS * LANES)
    wv = wkv[:, :, HEAD_DIM:].reshape(MLA_KV_LORA, GROUP_WIDTH)
    return wq.astype(BF16), wk.astype(BF16), wv.astype(BF16)


def kernel(x, positions, norm1, w_in, fox_f_bias, gla_gate_w2, gla_gate_b, gla_out_norm, mla_q_norm,
           mla_w_uq, mla_kv_norm, mla_w_ukv, ssm_conv_w, ssm_conv_b, ssm_dt_bias, ssm_A_log, ssm_D,
           ssm_norm, w_out, norm2, w_gate, w_up, w_down, final_norm):
    b, s, d = x.shape
    depth = w_in.shape[0]
    assert d == D_MODEL and s % max(ATT_TQ, TILE_ROWS) == 0 and ATT_TQ % ATT_TK == 0
    t = b * s
    x2 = x.reshape(t, d)

    half = MLA_ROPE // 2
    inv = ROPE_THETA ** (-jnp.arange(half, dtype=F32) / half)
    invf = jnp.zeros((1, LANES), F32).at[0, 64:64 + MLA_ROPE].set(jnp.concatenate([inv, inv]))
    cos_t, sin_t = _rope_tables(positions.reshape(t, 1), invf)

    tri_att = jnp.asarray(np.tril(np.ones((ATT_TK, ATT_TK), np.float32)), BF16)
    log2e = math.log2(math.e)
    c_fox = log2e
    c_mla = log2e * (HEAD_DIM + MLA_ROPE) ** -0.5

    for l in range(depth):
        fox, gla, mla, ssm, small = _inproj(x2, norm1[l][None, :], _prep_w_in(w_in[l]))

        qT, kA, vT = _fox_prep(fox, small, _lane_row(fox_f_bias[l], SMALL_FOX_F), tri_att, b, s)
        ya = _flash(qT, kA, vT, c_fox).reshape(b, GROUP_WIDTH, s)

        wq, wk, wv = _prep_mla_w(mla_w_uq[l], mla_w_ukv[l])
        qT, kA, vT = _mla_prep(mla, small, cos_t, sin_t, mla_q_norm[l][None, :], mla_kv_norm[l][None, :],
                               wq, wk, wv, b, s)
        yc = _flash(qT, kA, vT, c_mla).reshape(b, GROUP_WIDTH, s)

        w2p = jnp.zeros((LANES, HEADS * GLA_DK), F32).at[SMALL_GATE:SMALL_GATE + GLA_GATE_RANK].set(
            gla_gate_w2[l]).astype(BF16)
        yb = _gla(gla, small, w2p, gla_gate_b[l][None, :].astype(F32),
                  jnp.tile(gla_out_norm[l], HEADS)[None, :].astype(F32), b, s)

        yd = _ssd(ssm, small, ssm_conv_w[l], ssm_conv_b[l][None, :],
                  _lane_row(ssm_dt_bias[l], SMALL_DT), _lane_row(ssm_A_log[l], SMALL_DT),
                  jnp.repeat(ssm_D[l], HEAD_DIM)[None, :].astype(F32), ssm_norm[l][None, :], b, s)

        x2 = _out_ffn(x2, ya, yb, yc, yd, w_out[l].reshape(4, GROUP_WIDTH, D_MODEL).astype(BF16),
                      norm2[l][None, :], w_gate[l].astype(BF16), w_up[l].astype(BF16),
                      w_down[l].astype(BF16), final_norm[None, :], b, s,
                      final_norm=(l == depth - 1))
    return x2.reshape(b, s, d)
```

```python
import functools
import math

import numpy as np
import jax
import jax.numpy as jnp
from jax import lax
from jax.experimental import pallas as pl
from jax.experimental.pallas import tpu as pltpu

F32 = jnp.float32
BF16 = jnp.bfloat16

D_MODEL = 1024
GROUP_WIDTH = 256
HEADS = 4
HEAD_DIM = 64
GLA_DK = 32
GLA_GATE_RANK = 16
GLA_GATE_TAU = 16.0
GLA_CHUNK = 64
GLA_SUB = 16
MLA_ROPE = 32
MLA_Q_LORA = 256
MLA_KV_LORA = 128
ROPE_THETA = 10000.0
SSM_STATE = 128
SSM_CONV = 4
SSM_CHUNK = 128
SSM_CONV_DIM = 768
FFN_HIDDEN = 2816
EPS = 1e-6
NEG = -1e30

LANES = 128
QK_PAD = 128
VMEM_LIMIT = 56 * 1024 * 1024

SEG_FOX = (0, 768)
SEG_GLA = (768, 1536)
SEG_MLA = (1536, 1920)
SEG_SSM = (1920, 2944)
SEG_SMALL = (2944, 3072)
IN_PAD = 3072
SMALL_FOX_F = 0
SMALL_DT = 4
SMALL_GATE = 16
SMALL_KROPE = 64

TILE_ROWS = 512
ATT_TQ = 512
ATT_TK = 256
SCAN_ROWS = 256
FFN_TH = 1408


def _cparams(sem):
    return pltpu.CompilerParams(dimension_semantics=sem, vmem_limit_bytes=VMEM_LIMIT)


def _rms(x, g):
    ms = jnp.mean(x * x, axis=-1, keepdims=True)
    return x * lax.rsqrt(ms + EPS) * g


def _log_sigmoid(x):
    return jnp.minimum(x, 0.0) - jnp.log1p(jnp.exp(-jnp.abs(x)))


def _softplus(x):
    return jnp.maximum(x, 0.0) + jnp.log1p(jnp.exp(-jnp.abs(x)))


def _silu(x):
    return x / (1.0 + jnp.exp(-x))


def _split3(x):
    hi = x.astype(BF16)
    r = x - hi.astype(F32)
    mid = r.astype(BF16)
    lo = (r - mid.astype(F32)).astype(BF16)
    return hi, mid, lo


def _dot(a, b):
    return jnp.dot(a, b, preferred_element_type=F32)


def _dot_nt(a, b):
    return lax.dot_general(a, b, (((1,), (1,)), ((), ())), preferred_element_type=F32)


def _dot_tn(a, b):
    return lax.dot_general(a, b, (((0,), (0,)), ((), ())), preferred_element_type=F32)


def _exact_lhs_dot(m01, x):
    hi, mid, lo = _split3(x)
    return _dot(m01, hi) + _dot(m01, mid) + _dot(m01, lo)


def _inproj_body(x_ref, g_ref, w_ref, fox_ref, gla_ref, mla_ref, ssm_ref, small_ref):
    h = _rms(x_ref[...], g_ref[...]).astype(BF16)

    def seg(s):
        return _dot(h, w_ref[:, s[0]:s[1]])

    fox_ref[...] = seg(SEG_FOX).astype(BF16)
    gla_ref[...] = seg(SEG_GLA)
    mla_ref[...] = seg(SEG_MLA)
    ssm_ref[...] = seg(SEG_SSM)
    small_ref[...] = seg(SEG_SMALL)


def _inproj(x2, g, w):
    t = x2.shape[0]
    tm = TILE_ROWS
    widths = [s[1] - s[0] for s in (SEG_FOX, SEG_GLA, SEG_MLA, SEG_SSM, SEG_SMALL)]
    dts = [BF16, F32, F32, F32, F32]
    return pl.pallas_call(
        _inproj_body,
        grid=(t // tm,),
        in_specs=[pl.BlockSpec((tm, D_MODEL), lambda i: (i, 0)),
                  pl.BlockSpec((1, D_MODEL), lambda i: (0, 0)),
                  pl.BlockSpec((D_MODEL, IN_PAD), lambda i: (0, 0))],
        out_specs=[pl.BlockSpec((tm, w_), lambda i: (i, 0)) for w_ in widths],
        out_shape=[jax.ShapeDtypeStruct((t, w_), d_) for w_, d_ in zip(widths, dts)],
        compiler_params=_cparams(("parallel",)),
        name="inproj",
    )(x2, g, w)


def _rope_table_body(pos_ref, invf_ref, cos_ref, sin_ref):
    ang = pos_ref[...].astype(F32) * invf_ref[...]
    cos_ref[...] = jnp.cos(ang)
    sin_ref[...] = jnp.sin(ang)


def _rope_tables(pos_col, invf):
    t = pos_col.shape[0]
    tm = TILE_ROWS
    return pl.pallas_call(
        _rope_table_body,
        grid=(t // tm,),
        in_specs=[pl.BlockSpec((tm, 1), lambda i: (i, 0)),
                  pl.BlockSpec((1, LANES), lambda i: (0, 0))],
        out_specs=[pl.BlockSpec((tm, LANES), lambda i: (i, 0))] * 2,
        out_shape=[jax.ShapeDtypeStruct((t, LANES), F32)] * 2,
        compiler_params=_cparams(("parallel",)),
        name="rope_tables",
    )(pos_col, invf)


def _fox_prep_body(fox_ref, small_ref, fb_ref, tri_ref, qT_ref, kA_ref, vT_ref, carry_ref):
    ts = fox_ref.shape[0]

    @pl.when(pl.program_id(1) == 0)
    def _():
        carry_ref[...] = jnp.zeros_like(carry_ref)

    logf = _log_sigmoid(small_ref[...] + fb_ref[...])
    fcum = _exact_lhs_dot(tri_ref[...], logf) + carry_ref[...]
    carry_ref[...] = fcum[ts - 1:ts, :]
    f_hi, f_mid, f_lo = [p.astype(F32) for p in _split3(fcum)]

    lane = lax.broadcasted_iota(jnp.int32, (ts, LANES), 1)
    low = lane < HEAD_DIM
    for pair in range(2):
        qp = fox_ref[:, pair * LANES:(pair + 1) * LANES].astype(F32) * (HEAD_DIM ** -0.5)
        kp = fox_ref[:, 256 + pair * LANES:256 + (pair + 1) * LANES].astype(F32)
        vp = fox_ref[:, 512 + pair * LANES:512 + (pair + 1) * LANES].astype(F32)
        vT_ref[0, 0, pair * LANES:(pair + 1) * LANES, :] = vp.T.astype(BF16)
        for e in range(2):
            h = 2 * pair + e
            qh = qp if e == 0 else pltpu.roll(qp, HEAD_DIM, 1)
            kh = kp if e == 0 else pltpu.roll(kp, HEAD_DIM, 1)

            def col(src, dst):
                return pltpu.roll(src, (dst - h) % LANES, 1)

            qa = jnp.where(low, qh,
                 jnp.where(lane == 64, col(f_hi, 64),
                 jnp.where(lane == 65, col(f_mid, 65),
                 jnp.where(lane == 66, col(f_lo, 66),
                 jnp.where(lane < 70, 1.0, 0.0)))))
            ka = jnp.where(low, kh,
                 jnp.where(lane < 67, 1.0,
                 jnp.where(lane == 67, -col(f_hi, 67),
                 jnp.where(lane == 68, -col(f_mid, 68),
                 jnp.where(lane == 69, -col(f_lo, 69), 0.0)))))
            qT_ref[0, h] = qa.T.astype(BF16)
            kA_ref[0, h] = ka.astype(BF16)


def _att_out_shapes(b, s, ts):
    return [jax.ShapeDtypeStruct((b, HEADS, QK_PAD, s), BF16),
            jax.ShapeDtypeStruct((b, HEADS, s, QK_PAD), BF16),
            jax.ShapeDtypeStruct((b, s // ts, GROUP_WIDTH, ts), BF16)]


def _att_out_specs(ts):
    return [pl.BlockSpec((1, HEADS, QK_PAD, ts), lambda b, i: (b, 0, 0, i)),
            pl.BlockSpec((1, HEADS, ts, QK_PAD), lambda b, i: (b, 0, i, 0)),
            pl.BlockSpec((1, 1, GROUP_WIDTH, ts), lambda b, i: (b, i, 0, 0))]


def _fox_prep(fox, small, fbias, tri, b, s):
    ts = ATT_TK
    ns = s // ts
    return pl.pallas_call(
        _fox_prep_body,
        grid=(b, ns),
        in_specs=[pl.BlockSpec((ts, 768), lambda bi, i: (bi * ns + i, 0)),
                  pl.BlockSpec((ts, LANES), lambda bi, i: (bi * ns + i, 0)),
                  pl.BlockSpec((1, LANES), lambda bi, i: (0, 0)),
                  pl.BlockSpec((ts, ts), lambda bi, i: (0, 0))],
        out_specs=_att_out_specs(ts),
        out_shape=_att_out_shapes(b, s, ts),
        scratch_shapes=[pltpu.VMEM((1, LANES), F32)],
        compiler_params=_cparams(("parallel", "arbitrary")),
        name="fox_prep",
    )(fox, small, fbias, tri)


def _mla_prep_body(mla_ref, small_ref, cos_ref, sin_ref, qn_ref, kvn_ref, wq_ref, wk_ref, wv_ref,
                   qT_ref, kA_ref, vT_ref):
    ts = mla_ref.shape[0]
    lane = lax.broadcasted_iota(jnp.int32, (ts, LANES), 1)
    cosv = cos_ref[...]
    sinv = sin_ref[...]
    half = MLA_ROPE // 2
    sin_a = jnp.where((lane >= 64) & (lane < 64 + half), -sinv, 0.0)
    sin_b = jnp.where((lane >= 64 + half) & (lane < 64 + MLA_ROPE), sinv, 0.0)

    def rope(x):
        return (x * cosv + pltpu.roll(x, LANES - half, 1) * sin_a + pltpu.roll(x, half, 1) * sin_b)

    cq = _rms(mla_ref[:, 0:MLA_Q_LORA], qn_ref[...]).astype(BF16)
    ckv = _rms(mla_ref[:, MLA_Q_LORA:MLA_Q_LORA + MLA_KV_LORA], kvn_ref[...]).astype(BF16)
    kr = rope(jnp.where((lane >= 64) & (lane < 64 + MLA_ROPE), small_ref[...], 0.0))
    q_all = _dot(cq, wq_ref[...])
    k_all = _dot(ckv, wk_ref[...])
    v_all = _dot(ckv, wv_ref[...])
    for h in range(HEADS):
        qh = rope(q_all[:, h * LANES:(h + 1) * LANES])
        qT_ref[0, h] = qh.T.astype(BF16)
        kA_ref[0, h] = (k_all[:, h * LANES:(h + 1) * LANES] + kr).astype(BF16)
    for pair in range(2):
        vT_ref[0, 0, pair * LANES:(pair + 1) * LANES, :] = (
            v_all[:, pair * LANES:(pair + 1) * LANES].T.astype(BF16))


def _mla_prep(mla, small, cos_t, sin_t, qn, kvn, wq, wk, wv, b, s):
    ts = ATT_TK
    ns = s // ts
    row = lambda bi, i: (bi * ns + i, 0)
    full = lambda bi, i: (0, 0)
    return pl.pallas_call(
        _mla_prep_body,
        grid=(b, ns),
        in_specs=[pl.BlockSpec((ts, 384), row),
                  pl.BlockSpec((ts, LANES), row),
                  pl.BlockSpec((ts, LANES), row),
                  pl.BlockSpec((ts, LANES), row),
                  pl.BlockSpec((1, MLA_Q_LORA), full),
                  pl.BlockSpec((1, MLA_KV_LORA), full),
                  pl.BlockSpec((MLA_Q_LORA, HEADS * LANES), full),
                  pl.BlockSpec((MLA_KV_LORA, HEADS * LANES), full),
                  pl.BlockSpec((MLA_KV_LORA, GROUP_WIDTH), full)],
        out_specs=_att_out_specs(ts),
        out_shape=_att_out_shapes(b, s, ts),
        compiler_params=_cparams(("parallel", "parallel")),
        name="mla_prep",
    )(mla, small, cos_t, sin_t, qn, kvn, wq, wk, wv)


def _flash_body(qT_ref, k_ref, vT_ref, o_ref, m_ref, l_ref, acc_ref, s0, s1, p0, p1, a0, a1,
                *, tq, tk, c):
    i = pl.program_id(2)
    s_bufs, p_bufs, a_bufs = (s0, s1), (p0, p1), (a0, a1)
    qT = qT_ref[0, 0]
    m_ref[...] = jnp.full_like(m_ref, NEG)
    l_ref[...] = jnp.zeros_like(l_ref)
    acc_ref[...] = jnp.zeros_like(acc_ref)
    p_bufs[1][...] = jnp.zeros_like(p_bufs[1])
    a_bufs[1][...] = jnp.ones_like(a_bufs[1])
    per_q = tq // tk
    assert per_q == 2
    n_full = i * per_q

    def scores(j, slot):
        start = pl.multiple_of(j * tk, tk)
        s_bufs[slot][...] = _dot(k_ref[0, 0, pl.ds(start, tk), :], qT)

    def softmax(slot, diag):
        s = s_bufs[slot][...]
        if diag is not None:
            kv = lax.broadcasted_iota(jnp.int32, (tk, tq), 0) + diag * tk
            qi = lax.broadcasted_iota(jnp.int32, (tk, tq), 1)
            s = jnp.where(kv <= qi, s, NEG)
        m_prev = m_ref[...]
        m_new = jnp.maximum(m_prev, jnp.max(s, axis=0, keepdims=True))
        alpha = jnp.exp2((m_prev - m_new) * c)
        p = jnp.exp2((s - m_new) * c)
        l_ref[...] = alpha * l_ref[...] + jnp.sum(p, axis=0, keepdims=True)
        p_bufs[slot][...] = p.astype(BF16)
        a_bufs[slot][...] = alpha
        m_ref[...] = m_new

    def pv(j, slot):
        acc_ref[...] = a_bufs[slot][...] * acc_ref[...] + _dot(vT_ref[0, j, 0], p_bufs[slot][...])

    scores(0, 0)

    def body(t, carry):
        j = 2 * t
        scores(j + 1, 1)
        softmax(0, None)
        pv(jnp.maximum(j - 1, 0), 1)
        scores(j + 2, 0)
        softmax(1, None)
        pv(j, 0)
        return carry

    lax.fori_loop(0, i, body, 0)
    scores(n_full + 1, 1)
    softmax(0, 0)
    pv(jnp.maximum(n_full - 1, 0), 1)
    softmax(1, 1)
    pv(n_full, 0)
    pv(n_full + 1, 1)
    o_ref[0, 0] = (acc_ref[...] / l_ref[...]).astype(o_ref.dtype)


def _flash(qT, kA, vT, c):
    b, _, _, s = qT.shape
    tq, tk = ATT_TQ, ATT_TK
    nk = s // tk
    vT5 = vT.reshape(b, nk, HEADS, HEAD_DIM, tk)
    return pl.pallas_call(
        functools.partial(_flash_body, tq=tq, tk=tk, c=c),
        grid=(b, HEADS, s // tq),
        in_specs=[pl.BlockSpec((1, 1, QK_PAD, tq), lambda bi, h, i: (bi, h, 0, i)),
                  pl.BlockSpec((1, 1, s, QK_PAD), lambda bi, h, i: (bi, h, 0, 0)),
                  pl.BlockSpec((1, nk, 1, HEAD_DIM, tk), lambda bi, h, i: (bi, 0, h, 0, 0))],
        out_specs=pl.BlockSpec((1, 1, HEAD_DIM, tq), lambda bi, h, i: (bi, h, 0, i)),
        out_shape=jax.ShapeDtypeStruct((b, HEADS, HEAD_DIM, s), BF16),
        scratch_shapes=[pltpu.VMEM((1, tq), F32), pltpu.VMEM((1, tq), F32),
                        pltpu.VMEM((HEAD_DIM, tq), F32),
                        pltpu.VMEM((tk, tq), F32), pltpu.VMEM((tk, tq), F32),
                        pltpu.VMEM((tk, tq), BF16), pltpu.VMEM((tk, tq), BF16),
                        pltpu.VMEM((1, tq), F32), pltpu.VMEM((1, tq), F32)],
        compiler_params=_cparams(("parallel", "parallel", "arbitrary")),
        name="flash",
    )(qT, kA, vT5)


def _gla_consts(ts):
    idx = np.arange(ts)
    same = (idx[:, None] // GLA_CHUNK) == (idx[None, :] // GLA_CHUNK)
    tri = same & (idx[None, :] <= idx[:, None])
    sub_start = (idx // GLA_SUB) * GLA_SUB
    ref = same & (idx[None, :] < sub_start[:, None])
    mbig = np.concatenate([tri, ref, same], axis=0).astype(np.float32)
    hd = np.arange(HEADS * GLA_DK) // GLA_DK
    hv = np.arange(GROUP_WIDTH) // HEAD_DIM
    ind_dk_dv = (hd[:, None] == hv[None, :]).astype(np.float32)
    ind_mean = (hv[:, None] == hv[None, :]).astype(np.float32) / HEAD_DIM
    return (jnp.asarray(mbig, BF16), jnp.asarray(ind_dk_dv, BF16), jnp.asarray(ind_mean, BF16))


def _pick_heads(stacked, rows):
    lane_h = lax.broadcasted_iota(jnp.int32, (rows, GROUP_WIDTH), 1) // HEAD_DIM
    out = jnp.zeros((rows, GROUP_WIDTH), F32)
    for h in range(HEADS):
        out = jnp.where(lane_h == h, stacked[h * rows:(h + 1) * rows, :], out)
    return out


def _gla_body(gla_ref, small_ref, w2_ref, b2_ref, on_ref, mbig_ref, ind_ref, indm_ref, o_ref, st_ref):
    ts = gla_ref.shape[0]
    L = GLA_CHUNK
    nsub = L // GLA_SUB

    @pl.when(pl.program_id(1) == 0)
    def _():
        st_ref[...] = jnp.zeros_like(st_ref)

    q = gla_ref[:, 0:128] * (GLA_DK ** -0.5)
    k = gla_ref[:, 128:256]
    v = gla_ref[:, 256:512]
    r = gla_ref[:, 512:768]
    x = _dot(small_ref[...].astype(BF16), w2_ref[...]) + b2_ref[...]
    g = _log_sigmoid(x) * (1.0 / GLA_GATE_TAU)
    gg = _exact_lhs_dot(mbig_ref[...], g)
    G = gg[0:ts]
    R = gg[ts:2 * ts]
    GL = gg[2 * ts:3 * ts]
    q_inter = q * jnp.exp(G)
    q_norm = q * jnp.exp(G - R)
    k_dec = k * jnp.exp(GL - G)

    lane_h = lax.broadcasted_iota(jnp.int32, (L, 128), 1) // GLA_DK
    row = lax.broadcasted_iota(jnp.int32, (L, 128), 0)
    sub_pos = row % GLA_SUB
    row_blk = lax.broadcasted_iota(jnp.int32, (L, L * 2), 0) // GLA_SUB
    zeros_v = jnp.zeros((L, GROUP_WIDTH), BF16)

    def stack_heads(a):
        return jnp.concatenate([jnp.where(lane_h == h, a, 0.0) for h in range(HEADS)], axis=0)

    outs = []
    for c in range(ts // L):
        sl = slice(c * L, (c + 1) * L)
        Gc, kc, vc, qc = G[sl], k[sl], v[sl], q[sl]
        vc16 = vc.astype(BF16)

        kts = []
        for I in range(1, nsub):
            r_i = R[c * L + I * GLA_SUB:c * L + I * GLA_SUB + 1, :]
            kt = kc * jnp.exp(jnp.where(row < I * GLA_SUB, r_i - Gc, NEG))
            kts += [kt, jnp.zeros_like(kt)]
        kstack = jnp.concatenate(kts, axis=0).astype(BF16)
        sc = _dot_nt(stack_heads(q_norm[sl]).astype(BF16), kstack)
        a_rows = []
        for h in range(HEADS):
            sh = sc[h * L:(h + 1) * L]
            a_h = jnp.zeros((L, 2 * L), F32)
            for I in range(1, nsub):
                a_h = jnp.where(row_blk == I, sh[:, (I - 1) * 2 * L:I * 2 * L], a_h)
            a_rows.append(a_h)
        a_st = jnp.concatenate(a_rows, axis=0).astype(BF16)
        v_pad = jnp.concatenate([vc16, zeros_v], axis=0)
        st = st_ref[...]
        big = _dot(a_st, v_pad) + _dot_nt(stack_heads(q_inter[sl]).astype(BF16), st.astype(BF16))
        o_c = _pick_heads(big, L)

        ps = []
        for d in range(GLA_SUB):
            if d == 0:
                p = qc * kc
            else:
                arg = jnp.where(sub_pos >= d, Gc - pltpu.roll(Gc, d, 0), NEG)
                p = qc * pltpu.roll(kc, d, 0) * jnp.exp(arg)
            ps.append(p)
        abc = _dot(jnp.concatenate(ps, axis=0).astype(BF16), ind_ref[...])
        for d in range(GLA_SUB):
            vd = vc if d == 0 else pltpu.roll(vc, d, 0)
            o_c = o_c + abc[d * L:(d + 1) * L] * vd
        outs.append(o_c)

        st_ref[...] = (jnp.exp(GL[c * L:c * L + 1, :]) * st
                       + _dot_tn(vc16, k_dec[sl].astype(BF16)))

    o = jnp.concatenate(outs, axis=0)
    o2 = o * o
    hi = o2.astype(BF16)
    lo = (o2 - hi.astype(F32)).astype(BF16)
    ms = _dot(hi, indm_ref[...]) + _dot(lo, indm_ref[...])
    o_ref[...] = (o * lax.rsqrt(ms + EPS) * on_ref[...] * _silu(r)).astype(o_ref.dtype)


def _gla(gla, small, w2p, b2, onorm, b, s):
    ts = SCAN_ROWS
    ns = s // ts
    mbig, ind, indm = _gla_consts(ts)
    row = lambda bi, i: (bi * ns + i, 0)
    full = lambda bi, i: (0, 0)
    return pl.pallas_call(
        _gla_body,
        grid=(b, ns),
        in_specs=[pl.BlockSpec((ts, 768), row),
                  pl.BlockSpec((ts, LANES), row),
                  pl.BlockSpec((LANES, 128), full),
                  pl.BlockSpec((1, 128), full),
                  pl.BlockSpec((1, GROUP_WIDTH), full),
                  pl.BlockSpec((3 * ts, ts), full),
                  pl.BlockSpec((128, GROUP_WIDTH), full),
                  pl.BlockSpec((GROUP_WIDTH, GROUP_WIDTH), full)],
        out_specs=pl.BlockSpec((ts, GROUP_WIDTH), row),
        out_shape=jax.ShapeDtypeStruct((b * s, GROUP_WIDTH), BF16),
        scratch_shapes=[pltpu.VMEM((GROUP_WIDTH, 128), F32)],
        compiler_params=_cparams(("parallel", "arbitrary")),
        name="gla",
    )(gla, small, w2p, b2, onorm, mbig, ind, indm)


def _head_cols(src, lane0, rows, width, nheads):
    lane_h = lax.broadcasted_iota(jnp.int32, (rows, nheads * width), 1) // width
    out = jnp.broadcast_to(src[:, lane0:lane0 + 1], (rows, nheads * width))
    for h in range(1, nheads):
        out = jnp.where(lane_h == h, jnp.broadcast_to(src[:, lane0 + h:lane0 + h + 1],
                                                      (rows, nheads * width)), out)
    return out


def _ssd_body(ssm_ref, small_ref, cw_ref, cb_ref, dtb_ref, alog_ref, dskip_ref, nw_ref, tri_ref,
              o_ref, prev_ref, st_ref):
    ts = ssm_ref.shape[0]
    L = SSM_CHUNK

    @pl.when(pl.program_id(1) == 0)
    def _():
        prev_ref[...] = jnp.zeros_like(prev_ref)
        st_ref[...] = jnp.zeros_like(st_ref)

    z = ssm_ref[:, 0:GROUP_WIDTH]
    xbc = ssm_ref[:, GROUP_WIDTH:GROUP_WIDTH + SSM_CONV_DIM]
    prev8 = prev_ref[...]
    row8 = lax.broadcasted_iota(jnp.int32, (8, SSM_CONV_DIM), 0)
    conv = cb_ref[...] + cw_ref[SSM_CONV - 1:SSM_CONV, :] * xbc
    for sft in range(1, SSM_CONV):
        rolled = pltpu.roll(xbc, sft, 0)
        top = jnp.where(row8 < sft, pltpu.roll(prev8, sft, 0), rolled[0:8])
        shifted = jnp.concatenate([top, rolled[8:]], axis=0)
        conv = conv + cw_ref[SSM_CONV - 1 - sft:SSM_CONV - sft, :] * shifted
    prev_ref[...] = xbc[ts - 8:ts]
    xc = _silu(conv)
    xs = xc[:, 0:GROUP_WIDTH]
    bm = xc[:, GROUP_WIDTH:GROUP_WIDTH + 2 * SSM_STATE]
    cm = xc[:, GROUP_WIDTH + 2 * SSM_STATE:]

    dt = _softplus(small_ref[...] + dtb_ref[...])
    a = -jnp.exp(alog_ref[...]) * dt
    xdt = xs * _head_cols(dt, SMALL_DT, ts, HEAD_DIM, HEADS)

    ii = lax.broadcasted_iota(jnp.int32, (L, L), 0)
    jj = lax.broadcasted_iota(jnp.int32, (L, L), 1)
    lane_lo = jj < HEAD_DIM
    row_lo = ii < HEAD_DIM
    tri = tri_ref[...]
    ys = []
    for c in range(ts // L):
        sl = slice(c * L, (c + 1) * L)
        cs = _exact_lhs_dot(tri, a[sl])
        cs_t = cs.T
        cs_last = cs[L - 1:L, :]
        y_groups = []
        for grp in range(2):
            h0 = 2 * grp
            b_g = bm[sl, grp * SSM_STATE:(grp + 1) * SSM_STATE].astype(BF16)
            c_g = cm[sl, grp * SSM_STATE:(grp + 1) * SSM_STATE].astype(BF16)
            x_g = xdt[sl, grp * LANES:(grp + 1) * LANES]
            cb = _dot_nt(c_g, b_g)
            ws = []
            for h in (h0, h0 + 1):
                colh = cs[:, SMALL_DT + h:SMALL_DT + h + 1]
                rowh = cs_t[SMALL_DT + h:SMALL_DT + h + 1, :]
                ws.append(cb * jnp.exp(jnp.where(jj <= ii, colh - rowh, NEG)))
            yd = _dot(jnp.concatenate(ws, axis=0).astype(BF16), x_g.astype(BF16))
            y_diag = jnp.where(lane_lo, yd[0:L], yd[L:2 * L])
            cs_g = _head_cols(cs, SMALL_DT + h0, L, HEAD_DIM, 2)
            st_g = st_ref[grp * LANES:(grp + 1) * LANES, :]
            y_off = _dot_nt(c_g, st_g.astype(BF16)) * jnp.exp(cs_g)
            y_groups.append(y_diag + y_off)
            last_g = _head_cols(cs_last, SMALL_DT + h0, 1, HEAD_DIM, 2)
            x_dec = x_g * jnp.exp(last_g - cs_g)
            new = _dot_tn(x_dec.astype(BF16), b_g)
            e0 = jnp.exp(cs_last[:, SMALL_DT + h0:SMALL_DT + h0 + 1])
            e1 = jnp.exp(cs_last[:, SMALL_DT + h0 + 1:SMALL_DT + h0 + 2])
            st_ref[grp * LANES:(grp + 1) * LANES, :] = jnp.where(row_lo, e0, e1) * st_g + new
        ys.append(jnp.concatenate(y_groups, axis=1))
    y = jnp.concatenate(ys, axis=0)
    y = (y + dskip_ref[...] * xs) * _silu(z)
    halves = []
    for grp in range(2):
        yg = y[:, grp * LANES:(grp + 1) * LANES]
        halves.append(_rms(yg, nw_ref[:, grp * LANES:(grp + 1) * LANES]))
    o_ref[...] = jnp.concatenate(halves, axis=1).astype(o_ref.dtype)


def _ssd(ssm, small, cw, cb, dtb, alog, dskip, nw, b, s):
    ts = SCAN_ROWS
    ns = s // ts
    tri = jnp.asarray(np.tril(np.ones((SSM_CHUNK, SSM_CHUNK), np.float32)), BF16)
    row = lambda bi, i: (bi * ns + i, 0)
    full = lambda bi, i: (0, 0)
    return pl.pallas_call(
        _ssd_body,
        grid=(b, ns),
        in_specs=[pl.BlockSpec((ts, 1024), row),
                  pl.BlockSpec((ts, LANES), row),
                  pl.BlockSpec((SSM_CONV, SSM_CONV_DIM), full),
                  pl.BlockSpec((1, SSM_CONV_DIM), full),
                  pl.BlockSpec((1, LANES), full),
                  pl.BlockSpec((1, LANES), full),
                  pl.BlockSpec((1, GROUP_WIDTH), full),
                  pl.BlockSpec((1, GROUP_WIDTH), full),
                  pl.BlockSpec((SSM_CHUNK, SSM_CHUNK), full)],
        out_specs=pl.BlockSpec((ts, GROUP_WIDTH), row),
        out_shape=jax.ShapeDtypeStruct((b * s, GROUP_WIDTH), BF16),
        scratch_shapes=[pltpu.VMEM((8, SSM_CONV_DIM), F32), pltpu.VMEM((GROUP_WIDTH, SSM_STATE), F32)],
        compiler_params=_cparams(("parallel", "arbitrary")),
        name="ssd",
    )(ssm, small, cw, cb, dtb, alog, dskip, nw, tri)


def _out_ffn_body(x_ref, yaT_ref, yb_ref, ycT_ref, yd_ref, wo_ref, n2_ref, wg_ref, wu_ref, wd_ref,
                  fn_ref, o_ref, x1_ref, h2_ref, acc_ref, *, final_norm):
    kk = pl.program_id(1)

    @pl.when(kk == 0)
    def _():
        x1 = (x_ref[...]
              + _dot_tn(yaT_ref[0], wo_ref[0]) + _dot(yb_ref[...], wo_ref[1])
              + _dot_tn(ycT_ref[0], wo_ref[2]) + _dot(yd_ref[...], wo_ref[3]))
        x1_ref[...] = x1
        h2_ref[...] = _rms(x1, n2_ref[...]).astype(BF16)

    h2 = h2_ref[...]
    hidden = _silu(_dot(h2, wg_ref[...])) * _dot(h2, wu_ref[...])
    part = _dot(hidden.astype(BF16), wd_ref[...])

    @pl.when(kk == 0)
    def _():
        acc_ref[...] = part

    @pl.when(kk != 0)
    def _():
        acc_ref[...] += part

    @pl.when(kk == pl.num_programs(1) - 1)
    def _():
        out = x1_ref[...] + acc_ref[...]
        if final_norm:
            out = _rms(out, fn_ref[...])
        o_ref[...] = out


def _out_ffn(x2, yaT, yb, ycT, yd, wo, n2, wg, wu, wd, fn, b, s, final_norm):
    tm = TILE_ROWS
    th = FFN_TH
    ns = s // tm
    row = lambda i, k: (i, 0)
    full2 = lambda i, k: (0, 0)
    tr = lambda i, k: (i // ns, 0, i % ns)
    return pl.pallas_call(
        functools.partial(_out_ffn_body, final_norm=final_norm),
        grid=(b * ns, FFN_HIDDEN // th),
        in_specs=[pl.BlockSpec((tm, D_MODEL), row),
                  pl.BlockSpec((1, GROUP_WIDTH, tm), tr),
                  pl.BlockSpec((tm, GROUP_WIDTH), row),
                  pl.BlockSpec((1, GROUP_WIDTH, tm), tr),
                  pl.BlockSpec((tm, GROUP_WIDTH), row),
                  pl.BlockSpec((4, GROUP_WIDTH, D_MODEL), lambda i, k: (0, 0, 0)),
                  pl.BlockSpec((1, D_MODEL), full2),
                  pl.BlockSpec((D_MODEL, th), lambda i, k: (0, k)),
                  pl.BlockSpec((D_MODEL, th), lambda i, k: (0, k)),
                  pl.BlockSpec((th, D_MODEL), lambda i, k: (k, 0)),
                  pl.BlockSpec((1, D_MODEL), full2)],
        out_specs=pl.BlockSpec((tm, D_MODEL), row),
        out_shape=jax.ShapeDtypeStruct((b * s, D_MODEL), F32),
        scratch_shapes=[pltpu.VMEM((tm, D_MODEL), F32), pltpu.VMEM((tm, D_MODEL), BF16),
                        pltpu.VMEM((tm, D_MODEL), F32)],
        compiler_params=_cparams(("parallel", "arbitrary")),
        name="out_ffn",
    )(x2, yaT, yb, ycT, yd, wo, n2, wg, wu, wd, fn)


def _pad_cols(w, n):
    return jnp.pad(w, ((0, 0), (0, n - w.shape[1])))


def _prep_w_in(w):
    z = lambda n: jnp.zeros((w.shape[0], n), w.dtype)
    small = jnp.concatenate([w[:, 768:772], w[:, 2996:3000], z(8), w[:, 1540:1556], z(32),
                             w[:, 1940:1972], z(32)], axis=1)
    out = jnp.concatenate([w[:, 0:768], w[:, 772:1540], w[:, 1556:1940], w[:, 1972:2996], small], axis=1)
    return out.astype(BF16)


def _lane_row(vals, lane0):
    return jnp.zeros((1, LANES), F32).at[0, lane0:lane0 + vals.shape[0]].set(vals.astype(F32))


def _prep_mla_w(w_uq, w_ukv):
    wq = w_uq.reshape(MLA_Q_LORA, HEADS, HEAD_DIM + MLA_ROPE)
    wq = jnp.pad(wq, ((0, 0), (0, 0), (0, LANES - HEAD_DIM - MLA_ROPE))).reshape(MLA_Q_LORA, HEADS * LANES)
    wkv = w_ukv.reshape(MLA_KV_LORA, HEADS, 2 * HEAD_DIM)
    wk = jnp.pad(wkv[:, :, :HEAD_DIM], ((0, 0), (0, 0), (0, LANES - HEAD_DIM))).reshape(MLA_KV_LORA, HEADS * LANES)
    wv = wkv[:, :, HEAD_DIM:].reshape(MLA_KV_LORA, GROUP_WIDTH)
    return wq.astype(BF16), wk.astype(BF16), wv.astype(BF16)


def kernel(x, positions, norm1, w_in, fox_f_bias, gla_gate_w2, gla_gate_b, gla_out_norm, mla_q_norm,
           mla_w_uq, mla_kv_norm, mla_w_ukv, ssm_conv_w, ssm_conv_b, ssm_dt_bias, ssm_A_log, ssm_D,
           ssm_norm, w_out, norm2, w_gate, w_up, w_down, final_norm):
    b, s, d = x.shape
    depth = w_in.shape[0]
    assert d == D_MODEL and s % max(ATT_TQ, TILE_ROWS) == 0 and ATT_TQ % ATT_TK == 0
    t = b * s
    x2 = x.reshape(t, d)

    half = MLA_ROPE // 2
    inv = ROPE_THETA ** (-jnp.arange(half, dtype=F32) / half)
    invf = jnp.zeros((1, LANES), F32).at[0, 64:64 + MLA_ROPE].set(jnp.concatenate([inv, inv]))
    cos_t, sin_t = _rope_tables(positions.reshape(t, 1), invf)

    tri_att = jnp.asarray(np.tril(np.ones((ATT_TK, ATT_TK), np.float32)), BF16)
    log2e = math.log2(math.e)
    c_fox = log2e
    c_mla = log2e * (HEAD_DIM + MLA_ROPE) ** -0.5

    for l in range(depth):
        fox, gla, mla, ssm, small = _inproj(x2, norm1[l][None, :], _prep_w_in(w_in[l]))

        qT, kA, vT = _fox_prep(fox, small, _lane_row(fox_f_bias[l], SMALL_FOX_F), tri_att, b, s)
        ya = _flash(qT, kA, vT, c_fox).reshape(b, GROUP_WIDTH, s)

        wq, wk, wv = _prep_mla_w(mla_w_uq[l], mla_w_ukv[l])
        qT, kA, vT = _mla_prep(mla, small, cos_t, sin_t, mla_q_norm[l][None, :], mla_kv_norm[l][None, :],
                               wq, wk, wv, b, s)
        yc = _flash(qT, kA, vT, c_mla).reshape(b, GROUP_WIDTH, s)

        w2p = jnp.zeros((LANES, HEADS * GLA_DK), F32).at[SMALL_GATE:SMALL_GATE + GLA_GATE_RANK].set(
            gla_gate_w2[l]).astype(BF16)
        yb = _gla(gla, small, w2p, gla_gate_b[l][None, :].astype(F32),
                  jnp.tile(gla_out_norm[l], HEADS)[None, :].astype(F32), b, s)

        yd = _ssd(ssm, small, ssm_conv_w[l], ssm_conv_b[l][None, :],
                  _lane_row(ssm_dt_bias[l], SMALL_DT), _lane_row(ssm_A_log[l], SMALL_DT),
                  jnp.repeat(ssm_D[l], HEAD_DIM)[None, :].astype(F32), ssm_norm[l][None, :], b, s)

        x2 = _out_ffn(x2, ya, yb, yc, yd, w_out[l].reshape(4, GROUP_WIDTH, D_MODEL).astype(BF16),
                      norm2[l][None, :], w_gate[l].astype(BF16), w_up[l].astype(BF16),
                      w_down[l].astype(BF16), final_norm[None, :], b, s,
                      final_norm=(l == depth - 1))
    return x2.reshape(b, s, d)
```

```python
import functools
import math

import numpy as np
import jax
import jax.numpy as jnp
from jax import lax
from jax.experimental import pallas as pl
from jax.experimental.pallas import tpu as pltpu

F32 = jnp.float32
BF16 = jnp.bfloat16

D_MODEL = 1024
GROUP_WIDTH = 256
HEADS = 4
HEAD_DIM = 64
GLA_DK = 32
GLA_GATE_RANK = 16
GLA_GATE_TAU = 16.0
GLA_CHUNK = 64
GLA_SUB = 16
MLA_ROPE = 32
MLA_Q_LORA = 256
MLA_KV_LORA = 128
ROPE_THETA = 10000.0
SSM_STATE = 128
SSM_CONV = 4
SSM_CHUNK = 128
SSM_CONV_DIM = 768
FFN_HIDDEN = 2816
EPS = 1e-6
NEG = -1e30
LOG2E = math.log2(math.e)
SUM_ROWS = 16

LANES = 128
QK_PAD = 128
VMEM_LIMIT = 56 * 1024 * 1024

SEG_FOXQ = (0, 256)
SEG_FOXKV = (256, 768)
SEG_GLA = (768, 1536)
SEG_MLA = (1536, 1920)
SEG_SSM = (1920, 2944)
SEG_SMALL = (2944, 3072)
IN_PAD = 3072
SMALL_FOX_F = 0
SMALL_DT = 4
SMALL_GATE = 16
SMALL_KROPE = 64

TILE_ROWS = 512
ATT_TQ = 512
ATT_TK = 256
FLASH_UNROLL = 2
SCAN_ROWS = 256


def _cparams(sem):
    return pltpu.CompilerParams(dimension_semantics=sem, vmem_limit_bytes=VMEM_LIMIT)


def _rms(x, g):
    ms = jnp.mean(x * x, axis=-1, keepdims=True)
    return x * lax.rsqrt(ms + EPS) * g


def _log_sigmoid(x):
    return jnp.minimum(x, 0.0) - jnp.log1p(jnp.exp(-jnp.abs(x)))


def _softplus(x):
    return jnp.maximum(x, 0.0) + jnp.log1p(jnp.exp(-jnp.abs(x)))


def _silu(x):
    return x / (1.0 + jnp.exp(-x))


def _split3(x):
    hi = x.astype(BF16)
    r = x - hi.astype(F32)
    mid = r.astype(BF16)
    lo = (r - mid.astype(F32)).astype(BF16)
    return hi, mid, lo


def _dot(a, b):
    return jnp.dot(a, b, preferred_element_type=F32)


def _dot_nt(a, b):
    return lax.dot_general(a, b, (((1,), (1,)), ((), ())), preferred_element_type=F32)


def _dot_tn(a, b):
    return lax.dot_general(a, b, (((0,), (0,)), ((), ())), preferred_element_type=F32)


def _exact_lhs_dot(m01, x):
    hi, mid, lo = _split3(x)
    return _dot(m01, hi) + _dot(m01, mid) + _dot(m01, lo)


def _inproj_body(x_ref, g_ref, w_ref, foxq_ref, foxkv_ref, gla_ref, mla_ref, ssm_ref, small_ref):
    h = _rms(x_ref[...], g_ref[...]).astype(BF16)

    def seg(s):
        return _dot(h, w_ref[:, s[0]:s[1]])

    foxq_ref[...] = seg(SEG_FOXQ)
    foxkv_ref[...] = seg(SEG_FOXKV).astype(BF16)
    gla_ref[...] = seg(SEG_GLA)
    mla_ref[...] = seg(SEG_MLA)
    ssm_ref[...] = seg(SEG_SSM)
    small_ref[...] = seg(SEG_SMALL)


def _inproj(x2, g, w):
    t = x2.shape[0]
    tm = TILE_ROWS
    widths = [s[1] - s[0] for s in (SEG_FOXQ, SEG_FOXKV, SEG_GLA, SEG_MLA, SEG_SSM, SEG_SMALL)]
    dts = [F32, BF16, F32, F32, F32, F32]
    return pl.pallas_call(
        _inproj_body,
        grid=(t // tm,),
        in_specs=[pl.BlockSpec((tm, D_MODEL), lambda i: (i, 0)),
                  pl.BlockSpec((1, D_MODEL), lambda i: (0, 0)),
                  pl.BlockSpec((D_MODEL, IN_PAD), lambda i: (0, 0))],
        out_specs=[pl.BlockSpec((tm, w_), lambda i: (i, 0)) for w_ in widths],
        out_shape=[jax.ShapeDtypeStruct((t, w_), d_) for w_, d_ in zip(widths, dts)],
        compiler_params=_cparams(("parallel",)),
        name="inproj",
    )(x2, g, w)


def _rope_table_body(pos_ref, invf_ref, cos_ref, sin_ref):
    ang = pos_ref[...].astype(F32) * invf_ref[...]
    cos_ref[...] = jnp.cos(ang)
    sin_ref[...] = jnp.sin(ang)


def _rope_tables(pos_col, invf):
    t = pos_col.shape[0]
    tm = TILE_ROWS
    return pl.pallas_call(
        _rope_table_body,
        grid=(t // tm,),
        in_specs=[pl.BlockSpec((tm, 1), lambda i: (i, 0)),
                  pl.BlockSpec((1, LANES), lambda i: (0, 0))],
        out_specs=[pl.BlockSpec((tm, LANES), lambda i: (i, 0))] * 2,
        out_shape=[jax.ShapeDtypeStruct((t, LANES), F32)] * 2,
        compiler_params=_cparams(("parallel",)),
        name="rope_tables",
    )(pos_col, invf)


def _store_vT(vT_ref, v_all):
    for blk in range(v_all.shape[0] // ATT_TK):
        for pair in range(2):
            vp = v_all[blk * ATT_TK:(blk + 1) * ATT_TK, pair * LANES:(pair + 1) * LANES]
            vT_ref[0, 0, blk, pair * LANES:(pair + 1) * LANES, :] = vp.T.astype(BF16)


def _fox_prep_body(foxq_ref, foxkv_ref, small_ref, fb_ref, tri_ref, qT_ref, kA_ref, vT_ref, carry_ref):
    ts = foxq_ref.shape[0]

    @pl.when(pl.program_id(1) == 0)
    def _():
        carry_ref[...] = jnp.zeros_like(carry_ref)

    logf = _log_sigmoid(small_ref[...] + fb_ref[...])
    fcum = (_exact_lhs_dot(tri_ref[...], logf) + carry_ref[...])
    carry_ref[...] = fcum[ts - 1:ts, :]
    f_hi, f_mid, f_lo = [p.astype(F32) for p in _split3(fcum * LOG2E)]

    lane = lax.broadcasted_iota(jnp.int32, (ts, LANES), 1)
    low = lane < HEAD_DIM
    _store_vT(vT_ref, foxkv_ref[:, 256:512].astype(F32))
    for pair in range(2):
        qp = foxq_ref[:, pair * LANES:(pair + 1) * LANES] * (LOG2E * HEAD_DIM ** -0.5)
        kp = foxkv_ref[:, pair * LANES:(pair + 1) * LANES].astype(F32)
        for e in range(2):
            h = 2 * pair + e
            qh = qp if e == 0 else pltpu.roll(qp, HEAD_DIM, 1)
            kh = kp if e == 0 else pltpu.roll(kp, HEAD_DIM, 1)

            def col(src, dst):
                return pltpu.roll(src, (dst - h) % LANES, 1)

            qa = jnp.where(low, qh,
                 jnp.where(lane == 64, col(f_hi, 64),
                 jnp.where(lane == 65, col(f_mid, 65),
                 jnp.where(lane == 66, col(f_lo, 66),
                 jnp.where(lane < 70, 1.0, 0.0)))))
            ka = jnp.where(low, kh,
                 jnp.where(lane < 67, 1.0,
                 jnp.where(lane == 67, -col(f_hi, 67),
                 jnp.where(lane == 68, -col(f_mid, 68),
                 jnp.where(lane == 69, -col(f_lo, 69), 0.0)))))
            qT_ref[0, h] = qa.T.astype(BF16)
            kA_ref[0, h] = ka.astype(BF16)


def _att_out_shapes(b, s, ts):
    return [jax.ShapeDtypeStruct((b, HEADS, QK_PAD, s), BF16),
            jax.ShapeDtypeStruct((b, HEADS, s, QK_PAD), BF16),
            jax.ShapeDtypeStruct((b, s // ts, ts // ATT_TK, GROUP_WIDTH, ATT_TK), BF16)]


def _att_out_specs(ts):
    return [pl.BlockSpec((1, HEADS, QK_PAD, ts), lambda b, i: (b, 0, 0, i)),
            pl.BlockSpec((1, HEADS, ts, QK_PAD), lambda b, i: (b, 0, i, 0)),
            pl.BlockSpec((1, 1, ts // ATT_TK, GROUP_WIDTH, ATT_TK), lambda b, i: (b, i, 0, 0, 0))]


def _fox_prep(foxq, foxkv, small, fbias, tri, b, s):
    ts = TILE_ROWS
    ns = s // ts
    return pl.pallas_call(
        _fox_prep_body,
        grid=(b, ns),
        in_specs=[pl.BlockSpec((ts, 256), lambda bi, i: (bi * ns + i, 0)),
                  pl.BlockSpec((ts, 512), lambda bi, i: (bi * ns + i, 0)),
                  pl.BlockSpec((ts, LANES), lambda bi, i: (bi * ns + i, 0)),
                  pl.BlockSpec((1, LANES), lambda bi, i: (0, 0)),
                  pl.BlockSpec((ts, ts), lambda bi, i: (0, 0))],
        out_specs=_att_out_specs(ts),
        out_shape=_att_out_shapes(b, s, ts),
        scratch_shapes=[pltpu.VMEM((1, LANES), F32)],
        compiler_params=_cparams(("parallel", "arbitrary")),
        name="fox_prep",
    )(foxq, foxkv, small, fbias, tri)


def _mla_prep_body(mla_ref, small_ref, cos_ref, sin_ref, qn_ref, kvn_ref, wq_ref, wk_ref, wv_ref,
                   qT_ref, kA_ref, vT_ref):
    ts = mla_ref.shape[0]
    lane = lax.broadcasted_iota(jnp.int32, (ts, LANES), 1)
    cosv = cos_ref[...]
    sinv = sin_ref[...]
    half = MLA_ROPE // 2
    sin_a = jnp.where((lane >= 64) & (lane < 64 + half), -sinv, 0.0)
    sin_b = jnp.where((lane >= 64 + half) & (lane < 64 + MLA_ROPE), sinv, 0.0)

    def rope(x):
        return (x * cosv + pltpu.roll(x, LANES - half, 1) * sin_a + pltpu.roll(x, half, 1) * sin_b)

    cq = _rms(mla_ref[:, 0:MLA_Q_LORA], qn_ref[...]).astype(BF16)
    ckv = _rms(mla_ref[:, MLA_Q_LORA:MLA_Q_LORA + MLA_KV_LORA], kvn_ref[...]).astype(BF16)
    kr = rope(jnp.where((lane >= 64) & (lane < 64 + MLA_ROPE), small_ref[...], 0.0))
    q_all = _dot(cq, wq_ref[...]) * (LOG2E * (HEAD_DIM + MLA_ROPE) ** -0.5)
    k_all = _dot(ckv, wk_ref[...])
    for h in range(HEADS):
        qh = rope(q_all[:, h * LANES:(h + 1) * LANES])
        qT_ref[0, h] = qh.T.astype(BF16)
        kA_ref[0, h] = (k_all[:, h * LANES:(h + 1) * LANES] + kr).astype(BF16)
    _store_vT(vT_ref, _dot(ckv, wv_ref[...]))


def _mla_prep(mla, small, cos_t, sin_t, qn, kvn, wq, wk, wv, b, s):
    ts = TILE_ROWS
    ns = s // ts
    row = lambda bi, i: (bi * ns + i, 0)
    full = lambda bi, i: (0, 0)
    return pl.pallas_call(
        _mla_prep_body,
        grid=(b, ns),
        in_specs=[pl.BlockSpec((ts, 384), row),
                  pl.BlockSpec((ts, LANES), row),
                  pl.BlockSpec((ts, LANES), row),
                  pl.BlockSpec((ts, LANES), row),
                  pl.BlockSpec((1, MLA_Q_LORA), full),
                  pl.BlockSpec((1, MLA_KV_LORA), full),
                  pl.BlockSpec((MLA_Q_LORA, HEADS * LANES), full),
                  pl.BlockSpec((MLA_KV_LORA, HEADS * LANES), full),
                  pl.BlockSpec((MLA_KV_LORA, GROUP_WIDTH), full)],
        out_specs=_att_out_specs(ts),
        out_shape=_att_out_shapes(b, s, ts),
        compiler_params=_cparams(("parallel", "parallel")),
        name="mla_prep",
    )(mla, small, cos_t, sin_t, qn, kvn, wq, wk, wv)


def _flash_body(qT_ref, k_ref, vT_ref, o_ref, m_ref, acc_ref, s0, s1, p0, p1, a0, a1, *, tq, tk):
    i = pl.program_id(2)
    s_bufs, p_bufs, a_bufs = (s0, s1), (p0, p1), (a0, a1)
    m_ref[...] = jnp.full_like(m_ref, NEG)
    acc_ref[...] = jnp.zeros_like(acc_ref)
    p_bufs[1][...] = jnp.zeros_like(p_bufs[1])
    a_bufs[1][...] = jnp.ones_like(a_bufs[1])
    nb = tq // tk
    assert nb % 2 == 0
    n_full = nb * i
    ones = jnp.ones((SUM_ROWS, tk), BF16)

    def scores(j, slot, lo=0):
        start = pl.multiple_of(j * tk, tk)
        s_bufs[slot][:, lo:] = _dot(k_ref[0, 0, pl.ds(start, tk), :], qT_ref[0, 0, :, lo:])

    def softmax(slot, lo=0, diag=False):
        s = s_bufs[slot][:, lo:]
        if diag:
            keep = (lax.broadcasted_iota(jnp.int32, (tk, tk), 0)
                    <= lax.broadcasted_iota(jnp.int32, (tk, tk), 1))
            left = jnp.where(keep, s[:, :tk], NEG)
            s = left if s.shape[1] == tk else jnp.concatenate([left, s[:, tk:]], axis=1)
        m_prev = m_ref[:, lo:]
        m_new = jnp.maximum(m_prev, jnp.max(s, axis=0, keepdims=True))
        a_bufs[slot][:, lo:] = jnp.exp2(m_prev - m_new)
        p_bufs[slot][:, lo:] = jnp.exp2(s - m_new).astype(BF16)
        m_ref[:, lo:] = m_new

    def pv(j, slot, lo=0):
        v_ext = jnp.concatenate([vT_ref[0, j, 0], ones], axis=0)
        acc_ref[:, lo:] = (a_bufs[slot][:, lo:] * acc_ref[:, lo:]
                           + _dot(v_ext, p_bufs[slot][:, lo:]))

    scores(0, 0)

    def run_blocks(j, count):
        for u in range(count):
            scores(j + u + 1, (u + 1) % 2)
            softmax(u % 2)
            pv(jnp.maximum(j + u - 1, 0), (u + 1) % 2)

    per_trip = nb * FLASH_UNROLL
    n_trips = i // FLASH_UNROLL

    def body(t, carry):
        run_blocks(per_trip * t, per_trip)
        return carry

    lax.fori_loop(0, n_trips, body, 0)
    for r in range(1, FLASH_UNROLL):
        @pl.when(i % FLASH_UNROLL >= r)
        def _():
            run_blocks(per_trip * n_trips + nb * (r - 1), nb)
    for u in range(nb):
        if u + 1 < nb:
            scores(n_full + u + 1, (u + 1) % 2, lo=(u + 1) * tk)
        softmax(u % 2, lo=u * tk, diag=True)
        pv(jnp.maximum(n_full + u - 1, 0), (u + 1) % 2, lo=max(u - 1, 0) * tk)
    pv(n_full + nb - 1, (nb - 1) % 2, lo=(nb - 1) * tk)
    o_ref[0, 0] = (acc_ref[0:HEAD_DIM, :] / acc_ref[HEAD_DIM:HEAD_DIM + 1, :]).astype(o_ref.dtype)


def _flash(qT, kA, vT):
    b, _, _, s = qT.shape
    tq, tk = ATT_TQ, ATT_TK
    nk = s // tk
    vT5 = vT.reshape(b, nk, HEADS, HEAD_DIM, tk)
    return pl.pallas_call(
        functools.partial(_flash_body, tq=tq, tk=tk),
        grid=(b, HEADS, s // tq),
        in_specs=[pl.BlockSpec((1, 1, QK_PAD, tq), lambda bi, h, i: (bi, h, 0, i)),
                  pl.BlockSpec((1, 1, s, QK_PAD), lambda bi, h, i: (bi, h, 0, 0)),
                  pl.BlockSpec((1, nk, 1, HEAD_DIM, tk), lambda bi, h, i: (bi, 0, h, 0, 0))],
        out_specs=pl.BlockSpec((1, 1, HEAD_DIM, tq), lambda bi, h, i: (bi, h, 0, i)),
        out_shape=jax.ShapeDtypeStruct((b, HEADS, HEAD_DIM, s), BF16),
        scratch_shapes=[pltpu.VMEM((1, tq), F32),
                        pltpu.VMEM((HEAD_DIM + SUM_ROWS, tq), F32),
                        pltpu.VMEM((tk, tq), F32), pltpu.VMEM((tk, tq), F32),
                        pltpu.VMEM((tk, tq), BF16), pltpu.VMEM((tk, tq), BF16),
                        pltpu.VMEM((1, tq), F32), pltpu.VMEM((1, tq), F32)],
        compiler_params=_cparams(("parallel", "parallel", "arbitrary")),
        name="flash",
    )(qT, kA, vT5)


def _gla_consts(ts):
    idx = np.arange(ts)
    same = (idx[:, None] // GLA_CHUNK) == (idx[None, :] // GLA_CHUNK)
    tri = same & (idx[None, :] <= idx[:, None])
    sub_start = (idx // GLA_SUB) * GLA_SUB
    ref = same & (idx[None, :] < sub_start[:, None])
    mbig = np.concatenate([tri, ref, same], axis=0).astype(np.float32)
    hd = np.arange(HEADS * GLA_DK) // GLA_DK
    hv = np.arange(GROUP_WIDTH) // HEAD_DIM
    ind_dk_dv = (hd[:, None] == hv[None, :]).astype(np.float32)
    ind_mean = (hv[:, None] == hv[None, :]).astype(np.float32) / HEAD_DIM
    return (jnp.asarray(mbig, BF16), jnp.asarray(ind_dk_dv, BF16), jnp.asarray(ind_mean, BF16))


def _pick_heads(stacked, rows):
    lane_h = lax.broadcasted_iota(jnp.int32, (rows, GROUP_WIDTH), 1) // HEAD_DIM
    out = jnp.zeros((rows, GROUP_WIDTH), F32)
    for h in range(HEADS):
        out = jnp.where(lane_h == h, stacked[h * rows:(h + 1) * rows, :], out)
    return out


def _gla_body(gla_ref, small_ref, w2_ref, b2_ref, on_ref, mbig_ref, ind_ref, indm_ref, o_ref, st_ref):
    ts = gla_ref.shape[0]
    L = GLA_CHUNK
    nsub = L // GLA_SUB

    @pl.when(pl.program_id(1) == 0)
    def _():
        st_ref[...] = jnp.zeros_like(st_ref)

    q = gla_ref[:, 0:128] * (GLA_DK ** -0.5)
    k = gla_ref[:, 128:256]
    v = gla_ref[:, 256:512]
    r = gla_ref[:, 512:768]
    x = _dot(small_ref[...].astype(BF16), w2_ref[...]) + b2_ref[...]
    g = _log_sigmoid(x) * (1.0 / GLA_GATE_TAU)
    gg = _exact_lhs_dot(mbig_ref[...], g)
    G = gg[0:ts]
    R = gg[ts:2 * ts]
    GL = gg[2 * ts:3 * ts]
    q_inter = q * jnp.exp(G)
    q_norm = q * jnp.exp(G - R)
    k_dec = k * jnp.exp(GL - G)

    lane_h = lax.broadcasted_iota(jnp.int32, (L, 128), 1) // GLA_DK
    row = lax.broadcasted_iota(jnp.int32, (L, 128), 0)
    sub_pos = row % GLA_SUB
    row_blk = lax.broadcasted_iota(jnp.int32, (L, L * 2), 0) // GLA_SUB
    zeros_v = jnp.zeros((L, GROUP_WIDTH), BF16)

    def stack_heads(a):
        return jnp.concatenate([jnp.where(lane_h == h, a, 0.0) for h in range(HEADS)], axis=0)

    outs = []
    for c in range(ts // L):
        sl = slice(c * L, (c + 1) * L)
        Gc, kc, vc, qc = G[sl], k[sl], v[sl], q[sl]
        vc16 = vc.astype(BF16)

        kts = []
        for I in range(1, nsub):
            r_i = R[c * L + I * GLA_SUB:c * L + I * GLA_SUB + 1, :]
            kt = kc * jnp.exp(jnp.where(row < I * GLA_SUB, r_i - Gc, NEG))
            kts += [kt, jnp.zeros_like(kt)]
        kstack = jnp.concatenate(kts, axis=0).astype(BF16)
        sc = _dot_nt(stack_heads(q_norm[sl]).astype(BF16), kstack)
        a_rows = []
        for h in range(HEADS):
            sh = sc[h * L:(h + 1) * L]
            a_h = jnp.zeros((L, 2 * L), F32)
            for I in range(1, nsub):
                a_h = jnp.where(row_blk == I, sh[:, (I - 1) * 2 * L:I * 2 * L], a_h)
            a_rows.append(a_h)
        a_st = jnp.concatenate(a_rows, axis=0).astype(BF16)
        v_pad = jnp.concatenate([vc16, zeros_v], axis=0)
        st = st_ref[...]
        big = _dot(a_st, v_pad) + _dot_nt(stack_heads(q_inter[sl]).astype(BF16), st.astype(BF16))
        o_c = _pick_heads(big, L)

        ps = []
        for d in range(GLA_SUB):
            if d == 0:
                p = qc * kc
            else:
                arg = jnp.where(sub_pos >= d, Gc - pltpu.roll(Gc, d, 0), NEG)
                p = qc * pltpu.roll(kc, d, 0) * jnp.exp(arg)
            ps.append(p)
        abc = _dot(jnp.concatenate(ps, axis=0).astype(BF16), ind_ref[...])
        for d in range(GLA_SUB):
            vd = vc if d == 0 else pltpu.roll(vc, d, 0)
            o_c = o_c + abc[d * L:(d + 1) * L] * vd
        outs.append(o_c)

        st_ref[...] = (jnp.exp(GL[c * L:c * L + 1, :]) * st
                       + _dot_tn(vc16, k_dec[sl].astype(BF16)))

    o = jnp.concatenate(outs, axis=0)
    o2 = o * o
    hi = o2.astype(BF16)
    lo = (o2 - hi.astype(F32)).astype(BF16)
    ms = _dot(hi, indm_ref[...]) + _dot(lo, indm_ref[...])
    o_ref[...] = (o * lax.rsqrt(ms + EPS) * on_ref[...] * _silu(r)).astype(o_ref.dtype)


def _gla(gla, small, w2p, b2, onorm, b, s):
    ts = SCAN_ROWS
    ns = s // ts
    mbig, ind, indm = _gla_consts(ts)
    row = lambda bi, i: (bi * ns + i, 0)
    full = lambda bi, i: (0, 0)
    return pl.pallas_call(
        _gla_body,
        grid=(b, ns),
        in_specs=[pl.BlockSpec((ts, 768), row),
                  pl.BlockSpec((ts, LANES), row),
                  pl.BlockSpec((LANES, 128), full),
                  pl.BlockSpec((1, 128), full),
                  pl.BlockSpec((1, GROUP_WIDTH), full),
                  pl.BlockSpec((3 * ts, ts), full),
                  pl.BlockSpec((128, GROUP_WIDTH), full),
                  pl.BlockSpec((GROUP_WIDTH, GROUP_WIDTH), full)],
        out_specs=pl.BlockSpec((ts, GROUP_WIDTH), row),
        out_shape=jax.ShapeDtypeStruct((b * s, GROUP_WIDTH), BF16),
        scratch_shapes=[pltpu.VMEM((GROUP_WIDTH, 128), F32)],
        compiler_params=_cparams(("parallel", "arbitrary")),
        name="gla",
    )(gla, small, w2p, b2, onorm, mbig, ind, indm)


def _head_cols(src, lane0, rows, width, nheads):
    lane_h = lax.broadcasted_iota(jnp.int32, (rows, nheads * width), 1) // width
    out = jnp.broadcast_to(src[:, lane0:lane0 + 1], (rows, nheads * width))
    for h in range(1, nheads):
        out = jnp.where(lane_h == h, jnp.broadcast_to(src[:, lane0 + h:lane0 + h + 1],
                                                      (rows, nheads * width)), out)
    return out


def _ssd_body(ssm_ref, small_ref, cw_ref, cb_ref, dtb_ref, alog_ref, dskip_ref, nw_ref, tri_ref,
              o_ref, prev_ref, st_ref):
    ts = ssm_ref.shape[0]
    L = SSM_CHUNK

    @pl.when(pl.program_id(1) == 0)
    def _():
        prev_ref[...] = jnp.zeros_like(prev_ref)
        st_ref[...] = jnp.zeros_like(st_ref)

    z = ssm_ref[:, 0:GROUP_WIDTH]
    xbc = ssm_ref[:, GROUP_WIDTH:GROUP_WIDTH + SSM_CONV_DIM]
    prev8 = prev_ref[...]
    row8 = lax.broadcasted_iota(jnp.int32, (8, SSM_CONV_DIM), 0)
    conv = cb_ref[...] + cw_ref[SSM_CONV - 1:SSM_CONV, :] * xbc
    for sft in range(1, SSM_CONV):
        rolled = pltpu.roll(xbc, sft, 0)
        top = jnp.where(row8 < sft, pltpu.roll(prev8, sft, 0), rolled[0:8])
        shifted = jnp.concatenate([top, rolled[8:]], axis=0)
        conv = conv + cw_ref[SSM_CONV - 1 - sft:SSM_CONV - sft, :] * shifted
    prev_ref[...] = xbc[ts - 8:ts]
    xc = _silu(conv)
    xs = xc[:, 0:GROUP_WIDTH]
    bm = xc[:, GROUP_WIDTH:GROUP_WIDTH + 2 * SSM_STATE]
    cm = xc[:, GROUP_WIDTH + 2 * SSM_STATE:]

    dt = _softplus(small_ref[...] + dtb_ref[...])
    a = -jnp.exp(alog_ref[...]) * dt
    xdt = xs * _head_cols(dt, SMALL_DT, ts, HEAD_DIM, HEADS)

    ii = lax.broadcasted_iota(jnp.int32, (L, L), 0)
    jj = lax.broadcasted_iota(jnp.int32, (L, L), 1)
    lane_lo = jj < HEAD_DIM
    row_lo = ii < HEAD_DIM
    tri = tri_ref[...]
    ys = []
    for c in range(ts // L):
        sl = slice(c * L, (c + 1) * L)
        cs = _exact_lhs_dot(tri, a[sl])
        cs_t = cs.T
        cs_last = cs[L - 1:L, :]
        y_groups = []
        for grp in range(2):
            h0 = 2 * grp
            b_g = bm[sl, grp * SSM_STATE:(grp + 1) * SSM_STATE].astype(BF16)
            c_g = cm[sl, grp * SSM_STATE:(grp + 1) * SSM_STATE].astype(BF16)
            x_g = xdt[sl, grp * LANES:(grp + 1) * LANES]
            cb = _dot_nt(c_g, b_g)
            ws = []
            for h in (h0, h0 + 1):
                colh = cs[:, SMALL_DT + h:SMALL_DT + h + 1]
                rowh = cs_t[SMALL_DT + h:SMALL_DT + h + 1, :]
                ws.append(cb * jnp.exp(jnp.where(jj <= ii, colh - rowh, NEG)))
            yd = _dot(jnp.concatenate(ws, axis=0).astype(BF16), x_g.astype(BF16))
            y_diag = jnp.where(lane_lo, yd[0:L], yd[L:2 * L])
            cs_g = _head_cols(cs, SMALL_DT + h0, L, HEAD_DIM, 2)
            st_g = st_ref[grp * LANES:(grp + 1) * LANES, :]
            y_off = _dot_nt(c_g, st_g.astype(BF16)) * jnp.exp(cs_g)
            y_groups.append(y_diag + y_off)
            last_g = _head_cols(cs_last, SMALL_DT + h0, 1, HEAD_DIM, 2)
            x_dec = x_g * jnp.exp(last_g - cs_g)
            new = _dot_tn(x_dec.astype(BF16), b_g)
            e0 = jnp.exp(cs_last[:, SMALL_DT + h0:SMALL_DT + h0 + 1])
            e1 = jnp.exp(cs_last[:, SMALL_DT + h0 + 1:SMALL_DT + h0 + 2])
            st_ref[grp * LANES:(grp + 1) * LANES, :] = jnp.where(row_lo, e0, e1) * st_g + new
        ys.append(jnp.concatenate(y_groups, axis=1))
    y = jnp.concatenate(ys, axis=0)
    y = (y + dskip_ref[...] * xs) * _silu(z)
    halves = []
    for grp in range(2):
        yg = y[:, grp * LANES:(grp + 1) * LANES]
        halves.append(_rms(yg, nw_ref[:, grp * LANES:(grp + 1) * LANES]))
    o_ref[...] = jnp.concatenate(halves, axis=1).astype(o_ref.dtype)


def _ssd(ssm, small, cw, cb, dtb, alog, dskip, nw, b, s):
    ts = SCAN_ROWS
    ns = s // ts
    tri = jnp.asarray(np.tril(np.ones((SSM_CHUNK, SSM_CHUNK), np.float32)), BF16)
    row = lambda bi, i: (bi * ns + i, 0)
    full = lambda bi, i: (0, 0)
    return pl.pallas_call(
        _ssd_body,
        grid=(b, ns),
        in_specs=[pl.BlockSpec((ts, 1024), row),
                  pl.BlockSpec((ts, LANES), row),
                  pl.BlockSpec((SSM_CONV, SSM_CONV_DIM), full),
                  pl.BlockSpec((1, SSM_CONV_DIM), full),
                  pl.BlockSpec((1, LANES), full),
                  pl.BlockSpec((1, LANES), full),
                  pl.BlockSpec((1, GROUP_WIDTH), full),
                  pl.BlockSpec((1, GROUP_WIDTH), full),
                  pl.BlockSpec((SSM_CHUNK, SSM_CHUNK), full)],
        out_specs=pl.BlockSpec((ts, GROUP_WIDTH), row),
        out_shape=jax.ShapeDtypeStruct((b * s, GROUP_WIDTH), BF16),
        scratch_shapes=[pltpu.VMEM((8, SSM_CONV_DIM), F32), pltpu.VMEM((GROUP_WIDTH, SSM_STATE), F32)],
        compiler_params=_cparams(("parallel", "arbitrary")),
        name="ssd",
    )(ssm, small, cw, cb, dtb, alog, dskip, nw, tri)


def _out_ffn_body(x_ref, yaT_ref, yb_ref, ycT_ref, yd_ref, wo_ref, n2_ref, wg_ref, wu_ref, wd_ref,
                  fn_ref, o_ref, *, final_norm):
    x1 = (x_ref[...]
          + _dot_tn(yaT_ref[0], wo_ref[0]) + _dot(yb_ref[...], wo_ref[1])
          + _dot_tn(ycT_ref[0], wo_ref[2]) + _dot(yd_ref[...], wo_ref[3]))
    h2 = _rms(x1, n2_ref[...]).astype(BF16)
    hidden = _silu(_dot(h2, wg_ref[...])) * _dot(h2, wu_ref[...])
    out = x1 + _dot(hidden.astype(BF16), wd_ref[...])
    if final_norm:
        out = _rms(out, fn_ref[...])
    o_ref[...] = out


def _out_ffn(x2, yaT, yb, ycT, yd, wo, n2, wg, wu, wd, fn, b, s, final_norm):
    tm = TILE_ROWS
    ns = s // tm
    row = lambda i: (i, 0)
    tr = lambda i: (i // ns, 0, i % ns)

    def resident(shape):
        return pl.BlockSpec(shape, lambda i: (0,) * len(shape), pipeline_mode=pl.Buffered(1))

    return pl.pallas_call(
        functools.partial(_out_ffn_body, final_norm=final_norm),
        grid=(b * ns,),
        in_specs=[pl.BlockSpec((tm, D_MODEL), row),
                  pl.BlockSpec((1, GROUP_WIDTH, tm), tr),
                  pl.BlockSpec((tm, GROUP_WIDTH), row),
                  pl.BlockSpec((1, GROUP_WIDTH, tm), tr),
                  pl.BlockSpec((tm, GROUP_WIDTH), row),
                  resident((4, GROUP_WIDTH, D_MODEL)),
                  resident((1, D_MODEL)),
                  resident((D_MODEL, FFN_HIDDEN)),
                  resident((D_MODEL, FFN_HIDDEN)),
                  resident((FFN_HIDDEN, D_MODEL)),
                  resident((1, D_MODEL))],
        out_specs=pl.BlockSpec((tm, D_MODEL), row),
        out_shape=jax.ShapeDtypeStruct((b * s, D_MODEL), F32),
        compiler_params=_cparams(("parallel",)),
        name="out_ffn",
    )(x2, yaT, yb, ycT, yd, wo, n2, wg, wu, wd, fn)


def _prep_w_in(w):
    z = lambda n: jnp.zeros((w.shape[0], n), w.dtype)
    small = jnp.concatenate([w[:, 768:772], w[:, 2996:3000], z(8), w[:, 1540:1556], z(32),
                             w[:, 1940:1972], z(32)], axis=1)
    out = jnp.concatenate([w[:, 0:768], w[:, 772:1540], w[:, 1556:1940], w[:, 1972:2996], small], axis=1)
    return out.astype(BF16)


def _lane_row(vals, lane0):
    return jnp.zeros((1, LANES), F32).at[0, lane0:lane0 + vals.shape[0]].set(vals.astype(F32))


def _prep_mla_w(w_uq, w_ukv):
    wq = w_uq.reshape(MLA_Q_LORA, HEADS, HEAD_DIM + MLA_ROPE)
    wq = jnp.pad(wq, ((0, 0), (0, 0), (0, LANES - HEAD_DIM - MLA_ROPE))).reshape(MLA_Q_LORA, HEADS * LANES)
    wkv = w_ukv.reshape(MLA_KV_LORA, HEADS, 2 * HEAD_DIM)
    wk = jnp.pad(wkv[:, :, :HEAD_DIM], ((0, 0), (0, 0), (0, LANES - HEAD_DIM))).reshape(MLA_KV_LORA, HEADS * LANES)
    wv = wkv[:, :, HEAD_DIM:].reshape(MLA_KV_LORA, GROUP_WIDTH)
    return wq.astype(BF16), wk.astype(BF16), wv.astype(BF16)


def kernel(x, positions, norm1, w_in, fox_f_bias, gla_gate_w2, gla_gate_b, gla_out_norm, mla_q_norm,
           mla_w_uq, mla_kv_norm, mla_w_ukv, ssm_conv_w, ssm_conv_b, ssm_dt_bias, ssm_A_log, ssm_D,
           ssm_norm, w_out, norm2, w_gate, w_up, w_down, final_norm):
    b, s, d = x.shape
    depth = w_in.shape[0]
    assert d == D_MODEL and s % max(ATT_TQ, TILE_ROWS) == 0 and ATT_TQ % ATT_TK == 0
    t = b * s
    x2 = x.reshape(t, d)

    half = MLA_ROPE // 2
    inv = ROPE_THETA ** (-jnp.arange(half, dtype=F32) / half)
    invf = jnp.zeros((1, LANES), F32).at[0, 64:64 + MLA_ROPE].set(jnp.concatenate([inv, inv]))
    cos_t, sin_t = _rope_tables(positions.reshape(t, 1), invf)

    tri_att = jnp.asarray(np.tril(np.ones((TILE_ROWS, TILE_ROWS), np.float32)), BF16)

    for l in range(depth):
        foxq, foxkv, gla, mla, ssm, small = _inproj(x2, norm1[l][None, :], _prep_w_in(w_in[l]))

        qT, kA, vT = _fox_prep(foxq, foxkv, small, _lane_row(fox_f_bias[l], SMALL_FOX_F), tri_att, b, s)
        ya = _flash(qT, kA, vT).reshape(b, GROUP_WIDTH, s)

        wq, wk, wv = _prep_mla_w(mla_w_uq[l], mla_w_ukv[l])
        qT, kA, vT = _mla_prep(mla, small, cos_t, sin_t, mla_q_norm[l][None, :], mla_kv_norm[l][None, :],
                               wq, wk, wv, b, s)
        yc = _flash(qT, kA, vT).reshape(b, GROUP_WIDTH, s)

        w2p = jnp.zeros((LANES, HEADS * GLA_DK), F32).at[SMALL_GATE:SMALL_GATE + GLA_GATE_RANK].set(
            gla_gate_w2[l]).astype(BF16)
        yb = _gla(gla, small, w2p, gla_gate_b[l][None, :].astype(F32),
                  jnp.tile(gla_out_norm[l], HEADS)[None, :].astype(F32), b, s)

        yd = _ssd(ssm, small, ssm_conv_w[l], ssm_conv_b[l][None, :],
                  _lane_row(ssm_dt_bias[l], SMALL_DT), _lane_row(ssm_A_log[l], SMALL_DT),
                  jnp.repeat(ssm_D[l], HEAD_DIM)[None, :].astype(F32), ssm_norm[l][None, :], b, s)

        x2 = _out_ffn(x2, ya, yb, yc, yd, w_out[l].reshape(4, GROUP_WIDTH, D_MODEL).astype(BF16),
                      norm2[l][None, :], w_gate[l].astype(BF16), w_up[l].astype(BF16),
                      w_down[l].astype(BF16), final_norm[None, :], b, s,
                      final_norm=(l == depth - 1))
    return x2.reshape(b, s, d)
```

```python
import functools
import math

import numpy as np
import jax
import jax.numpy as jnp
from jax import lax
from jax.experimental import pallas as pl
from jax.experimental.pallas import tpu as pltpu

F32 = jnp.float32
BF16 = jnp.bfloat16

D_MODEL = 1024
GROUP_WIDTH = 256
HEADS = 4
HEAD_DIM = 64
GLA_DK = 32
GLA_GATE_RANK = 16
GLA_GATE_TAU = 16.0
GLA_CHUNK = 64
GLA_SUB = 16
MLA_ROPE = 32
MLA_Q_LORA = 256
MLA_KV_LORA = 128
ROPE_THETA = 10000.0
SSM_STATE = 128
SSM_CONV = 4
SSM_CHUNK = 128
SSM_CONV_DIM = 768
FFN_HIDDEN = 2816
EPS = 1e-6
NEG = -1e30
LOG2E = math.log2(math.e)
SUM_ROWS = 16

LANES = 128
QK_PAD = 128
VMEM_LIMIT = 56 * 1024 * 1024

SEG_FOXQ = (0, 256)
SEG_FOXKV = (256, 768)
SEG_GLA = (768, 1536)
SEG_MLA = (1536, 1920)
SEG_SSM = (1920, 2944)
SEG_SMALL = (2944, 3072)
IN_PAD = 3072
SMALL_FOX_F = 0
SMALL_DT = 4
SMALL_GATE = 16
SMALL_KROPE = 64

TILE_ROWS = 512
ATT_TQ = 512
ATT_TK = 256
FLASH_HEADS = 1
FLASH_UNROLL = 4
SCAN_ROWS = 512
GLA_ROWS = 512
CUMSUM_ROWS = 256


def _cparams(sem):
    return pltpu.CompilerParams(dimension_semantics=sem, vmem_limit_bytes=VMEM_LIMIT)


def _rms(x, g):
    ms = jnp.mean(x * x, axis=-1, keepdims=True)
    return x * lax.rsqrt(ms + EPS) * g


def _log_sigmoid(x):
    return jnp.minimum(x, 0.0) - jnp.log1p(jnp.exp(-jnp.abs(x)))


def _softplus(x):
    return jnp.maximum(x, 0.0) + jnp.log1p(jnp.exp(-jnp.abs(x)))


def _silu(x):
    return x / (1.0 + jnp.exp(-x))


def _split3(x):
    hi = x.astype(BF16)
    r = x - hi.astype(F32)
    mid = r.astype(BF16)
    lo = (r - mid.astype(F32)).astype(BF16)
    return hi, mid, lo


def _dot(a, b):
    return jnp.dot(a, b, preferred_element_type=F32)


def _dot_nt(a, b):
    return lax.dot_general(a, b, (((1,), (1,)), ((), ())), preferred_element_type=F32)


def _dot_tn(a, b):
    return lax.dot_general(a, b, (((0,), (0,)), ((), ())), preferred_element_type=F32)


def _exact_lhs_dot(m01, x, wide=True):
    hi, mid, lo = _split3(x)
    if not wide:
        return _dot(m01, hi) + _dot(m01, mid) + _dot(m01, lo)
    w = x.shape[1]
    y = _dot(m01, jnp.concatenate([hi, mid, lo], axis=1))
    return y[:, 0:w] + y[:, w:2 * w] + y[:, 2 * w:3 * w]


def _inproj_body(x_ref, g_ref, w_ref, foxq_ref, foxkv_ref, gla_ref, mla_ref, ssm_ref, small_ref):
    h = _rms(x_ref[...], g_ref[...]).astype(BF16)

    def seg(s):
        return _dot(h, w_ref[:, s[0]:s[1]])

    foxq_ref[...] = seg(SEG_FOXQ)
    foxkv_ref[...] = seg(SEG_FOXKV).astype(BF16)
    gla_ref[...] = seg(SEG_GLA)
    mla_ref[...] = seg(SEG_MLA)
    ssm_ref[...] = seg(SEG_SSM)
    small_ref[...] = seg(SEG_SMALL)


def _inproj(x2, g, w):
    t = x2.shape[0]
    tm = TILE_ROWS
    widths = [s[1] - s[0] for s in (SEG_FOXQ, SEG_FOXKV, SEG_GLA, SEG_MLA, SEG_SSM, SEG_SMALL)]
    dts = [F32, BF16, F32, F32, F32, F32]
    return pl.pallas_call(
        _inproj_body,
        grid=(t // tm,),
        in_specs=[pl.BlockSpec((tm, D_MODEL), lambda i: (i, 0)),
                  pl.BlockSpec((1, D_MODEL), lambda i: (0, 0)),
                  pl.BlockSpec((D_MODEL, IN_PAD), lambda i: (0, 0))],
        out_specs=[pl.BlockSpec((tm, w_), lambda i: (i, 0)) for w_ in widths],
        out_shape=[jax.ShapeDtypeStruct((t, w_), d_) for w_, d_ in zip(widths, dts)],
        compiler_params=_cparams(("parallel",)),
        name="inproj",
    )(x2, g, w)


def _rope_table_body(pos_ref, invf_ref, cos_ref, sin_ref):
    ang = pos_ref[...].astype(F32) * invf_ref[...]
    cos_ref[...] = jnp.cos(ang)
    sin_ref[...] = jnp.sin(ang)


def _rope_tables(pos_col, invf):
    t = pos_col.shape[0]
    tm = TILE_ROWS
    return pl.pallas_call(
        _rope_table_body,
        grid=(t // tm,),
        in_specs=[pl.BlockSpec((tm, 1), lambda i: (i, 0)),
                  pl.BlockSpec((1, LANES), lambda i: (0, 0))],
        out_specs=[pl.BlockSpec((tm, LANES), lambda i: (i, 0))] * 2,
        out_shape=[jax.ShapeDtypeStruct((t, LANES), F32)] * 2,
        compiler_params=_cparams(("parallel",)),
        name="rope_tables",
    )(pos_col, invf)


def _store_vT(vT_ref, v_all):
    for blk in range(v_all.shape[0] // ATT_TK):
        for pair in range(2):
            vp = v_all[blk * ATT_TK:(blk + 1) * ATT_TK, pair * LANES:(pair + 1) * LANES]
            vT_ref[0, 0, blk, pair * LANES:(pair + 1) * LANES, :] = vp.T.astype(BF16)


def _fox_prep_body(foxq_ref, foxkv_ref, small_ref, fb_ref, tri_ref, qT_ref, kA_ref, vT_ref, carry_ref):
    ts = foxq_ref.shape[0]

    @pl.when(pl.program_id(1) == 0)
    def _():
        carry_ref[...] = jnp.zeros_like(carry_ref)

    logf = _log_sigmoid(small_ref[...] + fb_ref[...])
    blk = tri_ref.shape[0]
    pieces, run = [], carry_ref[...]
    for i0 in range(0, ts, blk):
        part = _exact_lhs_dot(tri_ref[...], logf[i0:i0 + blk]) + run
        run = part[blk - 1:blk, :]
        pieces.append(part)
    fcum = jnp.concatenate(pieces, axis=0)
    carry_ref[...] = run
    f_hi, f_mid, f_lo = [p.astype(F32) for p in _split3(fcum * LOG2E)]

    lane = lax.broadcasted_iota(jnp.int32, (ts, LANES), 1)
    low = lane < HEAD_DIM
    _store_vT(vT_ref, foxkv_ref[:, 256:512].astype(F32))
    for pair in range(2):
        qp = foxq_ref[:, pair * LANES:(pair + 1) * LANES] * (LOG2E * HEAD_DIM ** -0.5)
        kp = foxkv_ref[:, pair * LANES:(pair + 1) * LANES].astype(F32)
        for e in range(2):
            h = 2 * pair + e
            qh = qp if e == 0 else pltpu.roll(qp, HEAD_DIM, 1)
            kh = kp if e == 0 else pltpu.roll(kp, HEAD_DIM, 1)

            def col(src, dst):
                return pltpu.roll(src, (dst - h) % LANES, 1)

            qa = jnp.where(low, qh,
                 jnp.where(lane == 64, col(f_hi, 64),
                 jnp.where(lane == 65, col(f_mid, 65),
                 jnp.where(lane == 66, col(f_lo, 66),
                 jnp.where(lane < 70, 1.0, 0.0)))))
            ka = jnp.where(low, kh,
                 jnp.where(lane < 67, 1.0,
                 jnp.where(lane == 67, -col(f_hi, 67),
                 jnp.where(lane == 68, -col(f_mid, 68),
                 jnp.where(lane == 69, -col(f_lo, 69), 0.0)))))
            qT_ref[0, h] = qa.T.astype(BF16)
            kA_ref[0, h] = ka.astype(BF16)


def _att_out_shapes(b, s, ts):
    return [jax.ShapeDtypeStruct((b, HEADS, QK_PAD, s), BF16),
            jax.ShapeDtypeStruct((b, HEADS, s, QK_PAD), BF16),
            jax.ShapeDtypeStruct((b, s // ts, ts // ATT_TK, GROUP_WIDTH, ATT_TK), BF16)]


def _att_out_specs(ts):
    return [pl.BlockSpec((1, HEADS, QK_PAD, ts), lambda b, i: (b, 0, 0, i)),
            pl.BlockSpec((1, HEADS, ts, QK_PAD), lambda b, i: (b, 0, i, 0)),
            pl.BlockSpec((1, 1, ts // ATT_TK, GROUP_WIDTH, ATT_TK), lambda b, i: (b, i, 0, 0, 0))]


def _fox_prep(foxq, foxkv, small, fbias, tri, b, s):
    ts = TILE_ROWS
    ns = s // ts
    return pl.pallas_call(
        _fox_prep_body,
        grid=(b, ns),
        in_specs=[pl.BlockSpec((ts, 256), lambda bi, i: (bi * ns + i, 0)),
                  pl.BlockSpec((ts, 512), lambda bi, i: (bi * ns + i, 0)),
                  pl.BlockSpec((ts, LANES), lambda bi, i: (bi * ns + i, 0)),
                  pl.BlockSpec((1, LANES), lambda bi, i: (0, 0)),
                  pl.BlockSpec(tri.shape, lambda bi, i: (0, 0))],
        out_specs=_att_out_specs(ts),
        out_shape=_att_out_shapes(b, s, ts),
        scratch_shapes=[pltpu.VMEM((1, LANES), F32)],
        compiler_params=_cparams(("parallel", "arbitrary")),
        name="fox_prep",
    )(foxq, foxkv, small, fbias, tri)


def _mla_prep_body(mla_ref, small_ref, cos_ref, sin_ref, qn_ref, kvn_ref, wq_ref, wk_ref, wv_ref,
                   qT_ref, kA_ref, vT_ref):
    ts = mla_ref.shape[0]
    lane = lax.broadcasted_iota(jnp.int32, (ts, LANES), 1)
    cosv = cos_ref[...]
    sinv = sin_ref[...]
    half = MLA_ROPE // 2
    sin_a = jnp.where((lane >= 64) & (lane < 64 + half), -sinv, 0.0)
    sin_b = jnp.where((lane >= 64 + half) & (lane < 64 + MLA_ROPE), sinv, 0.0)

    def rope(x):
        return (x * cosv + pltpu.roll(x, LANES - half, 1) * sin_a + pltpu.roll(x, half, 1) * sin_b)

    cq = _rms(mla_ref[:, 0:MLA_Q_LORA], qn_ref[...]).astype(BF16)
    ckv = _rms(mla_ref[:, MLA_Q_LORA:MLA_Q_LORA + MLA_KV_LORA], kvn_ref[...]).astype(BF16)
    kr = rope(jnp.where((lane >= 64) & (lane < 64 + MLA_ROPE), small_ref[...], 0.0))
    q_all = _dot(cq, wq_ref[...]) * (LOG2E * (HEAD_DIM + MLA_ROPE) ** -0.5)
    k_all = _dot(ckv, wk_ref[...])
    for h in range(HEADS):
        qh = rope(q_all[:, h * LANES:(h + 1) * LANES])
        qT_ref[0, h] = qh.T.astype(BF16)
        kA_ref[0, h] = (k_all[:, h * LANES:(h + 1) * LANES] + kr).astype(BF16)
    _store_vT(vT_ref, _dot(ckv, wv_ref[...]))


def _mla_prep(mla, small, cos_t, sin_t, qn, kvn, wq, wk, wv, b, s):
    ts = TILE_ROWS
    ns = s // ts
    row = lambda bi, i: (bi * ns + i, 0)
    full = lambda bi, i: (0, 0)
    return pl.pallas_call(
        _mla_prep_body,
        grid=(b, ns),
        in_specs=[pl.BlockSpec((ts, 384), row),
                  pl.BlockSpec((ts, LANES), row),
                  pl.BlockSpec((ts, LANES), row),
                  pl.BlockSpec((ts, LANES), row),
                  pl.BlockSpec((1, MLA_Q_LORA), full),
                  pl.BlockSpec((1, MLA_KV_LORA), full),
                  pl.BlockSpec((MLA_Q_LORA, HEADS * LANES), full),
                  pl.BlockSpec((MLA_KV_LORA, HEADS * LANES), full),
                  pl.BlockSpec((MLA_KV_LORA, GROUP_WIDTH), full)],
        out_specs=_att_out_specs(ts),
        out_shape=_att_out_shapes(b, s, ts),
        compiler_params=_cparams(("parallel", "parallel")),
        name="mla_prep",
    )(mla, small, cos_t, sin_t, qn, kvn, wq, wk, wv)


def _flash_body(qT_ref, k_ref, vT_ref, o_ref, m_ref, acc_ref, s_ref, p_ref, a_ref, mb_ref, *, tq, tk):
    i = pl.program_id(2)
    streams = range(FLASH_HEADS)
    m_ref[...] = jnp.full_like(m_ref, NEG)
    acc_ref[...] = jnp.zeros_like(acc_ref)
    p_ref[:, 1] = jnp.zeros_like(p_ref[:, 1])
    a_ref[:, 1] = jnp.ones_like(a_ref[:, 1])
    nb = tq // tk
    assert nb % 2 == 0
    n_full = nb * i
    ones = jnp.ones((SUM_ROWS, tk), BF16)

    def scores(h, j, slot, lo=0, with_max=True):
        start = pl.multiple_of(j * tk, tk)
        s = _dot(k_ref[0, h, pl.ds(start, tk), :], qT_ref[0, h, :, lo:])
        s_ref[h, slot, :, lo:] = s
        if with_max:
            mb_ref[h, slot] = jnp.max(s, axis=0, keepdims=True)

    def softmax(h, slot, lo=0, diag=False):
        s = s_ref[h, slot, :, lo:]
        if diag:
            keep = (lax.broadcasted_iota(jnp.int32, (tk, tk), 0)
                    <= lax.broadcasted_iota(jnp.int32, (tk, tk), 1))
            left = jnp.where(keep, s[:, :tk], NEG)
            s = left if s.shape[1] == tk else jnp.concatenate([left, s[:, tk:]], axis=1)
            m_blk = jnp.max(s, axis=0, keepdims=True)
        else:
            m_blk = mb_ref[h, slot]
        m_prev = m_ref[h, :, lo:]
        m_new = jnp.maximum(m_prev, m_blk)
        a_ref[h, slot, :, lo:] = jnp.exp2(m_prev - m_new)
        p_ref[h, slot, :, lo:] = jnp.exp2(s - m_new).astype(BF16)
        m_ref[h, :, lo:] = m_new

    def pv(h, j, slot, lo=0):
        v_ext = jnp.concatenate([vT_ref[0, j, h], ones], axis=0)
        acc_ref[h, :, lo:] = (a_ref[h, slot, :, lo:] * acc_ref[h, :, lo:]
                              + _dot(v_ext, p_ref[h, slot, :, lo:]))

    for h in streams:
        scores(h, 0, 0)

    def run_blocks(j, count):
        for u in range(count):
            for h in streams:
                scores(h, j + u + 1, (u + 1) % 2)
                softmax(h, u % 2)
                pv(h, jnp.maximum(j + u - 1, 0), (u + 1) % 2)

    per_trip = nb * FLASH_UNROLL
    n_trips = i // FLASH_UNROLL

    def body(t, carry):
        run_blocks(per_trip * t, per_trip)
        return carry

    lax.fori_loop(0, n_trips, body, 0)
    for r in range(1, FLASH_UNROLL):
        @pl.when(i % FLASH_UNROLL >= r)
        def _():
            run_blocks(per_trip * n_trips + nb * (r - 1), nb)
    for u in range(nb):
        for h in streams:
            if u + 1 < nb:
                scores(h, n_full + u + 1, (u + 1) % 2, lo=(u + 1) * tk, with_max=False)
            softmax(h, u % 2, lo=u * tk, diag=True)
            pv(h, jnp.maximum(n_full + u - 1, 0), (u + 1) % 2, lo=max(u - 1, 0) * tk)
    for h in streams:
        pv(h, n_full + nb - 1, (nb - 1) % 2, lo=(nb - 1) * tk)
        o_ref[0, h] = (acc_ref[h, 0:HEAD_DIM, :] / acc_ref[h, HEAD_DIM:HEAD_DIM + 1, :]).astype(o_ref.dtype)


def _flash(qT, kA, vT):
    b, _, _, s = qT.shape
    tq, tk = ATT_TQ, ATT_TK
    nk = s // tk
    fh = FLASH_HEADS
    vT5 = vT.reshape(b, nk, HEADS, HEAD_DIM, tk)
    return pl.pallas_call(
        functools.partial(_flash_body, tq=tq, tk=tk),
        grid=(b, HEADS // fh, s // tq),
        in_specs=[pl.BlockSpec((1, fh, QK_PAD, tq), lambda bi, h, i: (bi, h, 0, i)),
                  pl.BlockSpec((1, fh, s, QK_PAD), lambda bi, h, i: (bi, h, 0, 0)),
                  pl.BlockSpec((1, nk, fh, HEAD_DIM, tk), lambda bi, h, i: (bi, 0, h, 0, 0))],
        out_specs=pl.BlockSpec((1, fh, HEAD_DIM, tq), lambda bi, h, i: (bi, h, 0, i)),
        out_shape=jax.ShapeDtypeStruct((b, HEADS, HEAD_DIM, s), BF16),
        scratch_shapes=[pltpu.VMEM((fh, 1, tq), F32),
                        pltpu.VMEM((fh, HEAD_DIM + SUM_ROWS, tq), F32),
                        pltpu.VMEM((fh, 2, tk, tq), F32),
                        pltpu.VMEM((fh, 2, tk, tq), BF16),
                        pltpu.VMEM((fh, 2, 1, tq), F32),
                        pltpu.VMEM((fh, 2, 1, tq), F32)],
        compiler_params=_cparams(("parallel", "parallel", "arbitrary")),
        name="flash",
    )(qT, kA, vT5)


def _gla_consts(ts):
    idx = np.arange(ts)
    same = (idx[:, None] // GLA_CHUNK) == (idx[None, :] // GLA_CHUNK)
    mbig = (same & (idx[None, :] <= idx[:, None])).astype(np.float32)
    hd = np.arange(HEADS * GLA_DK) // GLA_DK
    hv = np.arange(GROUP_WIDTH) // HEAD_DIM
    ind_dk_dv = (hd[:, None] == hv[None, :]).astype(np.float32)
    ind_mean = (hv[:, None] == hv[None, :]).astype(np.float32) / HEAD_DIM
    return (jnp.asarray(mbig, BF16), jnp.asarray(ind_dk_dv, BF16), jnp.asarray(ind_mean, BF16))


def _pick_heads(stacked, rows):
    lane_h = lax.broadcasted_iota(jnp.int32, (rows, GROUP_WIDTH), 1) // HEAD_DIM
    out = jnp.zeros((rows, GROUP_WIDTH), F32)
    for h in range(HEADS):
        out = jnp.where(lane_h == h, stacked[h * rows:(h + 1) * rows, :], out)
    return out


def _gla_body(gla_ref, small_ref, w2_ref, b2_ref, on_ref, mbig_ref, ind_ref, indm_ref, o_ref, st_ref,
              gpad_ref, kpad_ref, vpad_ref):
    ts = gla_ref.shape[0]
    L = GLA_CHUNK
    nsub = L // GLA_SUB
    PAD = GLA_SUB

    @pl.when(pl.program_id(1) == 0)
    def _():
        st_ref[...] = jnp.zeros_like(st_ref)

    q = gla_ref[:, 0:128] * (GLA_DK ** -0.5)
    k = gla_ref[:, 128:256]
    v = gla_ref[:, 256:512]
    r = gla_ref[:, 512:768]
    x = _dot(small_ref[...].astype(BF16), w2_ref[...]) + b2_ref[...]
    g = _log_sigmoid(x) * (1.0 / GLA_GATE_TAU)
    cb = mbig_ref.shape[0]
    G = jnp.concatenate([_exact_lhs_dot(mbig_ref[...], g[i0:i0 + cb]) for i0 in range(0, ts, cb)], axis=0)
    r_rows, gl_rows = [], []
    for c in range(ts // L):
        for I in range(nsub):
            at = c * L + I * GLA_SUB
            r_rows.append(jnp.zeros((GLA_SUB, 128), F32) if I == 0
                          else jnp.broadcast_to(G[at - 1:at, :], (GLA_SUB, 128)))
        gl_rows.append(jnp.broadcast_to(G[(c + 1) * L - 1:(c + 1) * L, :], (L, 128)))
    R = jnp.concatenate(r_rows, axis=0)
    GL = jnp.concatenate(gl_rows, axis=0)
    q_inter = q * jnp.exp(G)
    q_norm = q * jnp.exp(G - R)
    k_dec = k * jnp.exp(GL - G)
    for ref, val in ((gpad_ref, G), (kpad_ref, k), (vpad_ref, v)):
        ref[0:PAD, :] = jnp.zeros((PAD, val.shape[1]), F32)
        ref[PAD:PAD + ts, :] = val

    lane_h = lax.broadcasted_iota(jnp.int32, (L, 128), 1) // GLA_DK
    row = lax.broadcasted_iota(jnp.int32, (L, 128), 0)
    sub_pos = row % GLA_SUB
    sub_idx = row // GLA_SUB
    row_blk = lax.broadcasted_iota(jnp.int32, (L, L * 2), 0) // GLA_SUB
    zeros_v = jnp.zeros((L, GROUP_WIDTH), BF16)

    def stack_heads(a):
        return jnp.concatenate([jnp.where(lane_h == h, a, 0.0) for h in range(HEADS)], axis=0)

    outs = []
    for c in range(ts // L):
        sl = slice(c * L, (c + 1) * L)
        Gc, kc, vc, qc = G[sl], k[sl], v[sl], q[sl]
        vc16 = vc.astype(BF16)

        r_next = jnp.concatenate([R[c * L + GLA_SUB:(c + 1) * L], R[(c + 1) * L - GLA_SUB:(c + 1) * L]],
                                 axis=0)
        base = kc * jnp.exp(jnp.where(sub_idx < nsub - 1, r_next - Gc, NEG))
        kts = []
        kt = jnp.zeros_like(base)
        for I in range(1, nsub):
            if I > 1:
                r_i = R[c * L + I * GLA_SUB:c * L + I * GLA_SUB + 1, :]
                r_p = R[c * L + (I - 1) * GLA_SUB:c * L + (I - 1) * GLA_SUB + 1, :]
                kt = kt * jnp.exp(r_i - r_p)
            kt = jnp.where(sub_idx == I - 1, base, kt)
            kts += [kt, jnp.zeros_like(kt)]
        kstack = jnp.concatenate(kts, axis=0).astype(BF16)
        sc = _dot_nt(stack_heads(q_norm[sl]).astype(BF16), kstack)
        a_rows = []
        for h in range(HEADS):
            sh = sc[h * L:(h + 1) * L]
            a_h = jnp.zeros((L, 2 * L), F32)
            for I in range(1, nsub):
                a_h = jnp.where(row_blk == I, sh[:, (I - 1) * 2 * L:I * 2 * L], a_h)
            a_rows.append(a_h)
        a_st = jnp.concatenate(a_rows, axis=0).astype(BF16)
        v_pad = jnp.concatenate([vc16, zeros_v], axis=0)
        st = st_ref[...]
        big = _dot(a_st, v_pad) + _dot_nt(stack_heads(q_inter[sl]).astype(BF16), st.astype(BF16))
        o_c = _pick_heads(big, L)

        ps = []
        for d in range(GLA_SUB):
            if d == 0:
                p = qc * kc
            else:
                lo = PAD + c * L - d
                arg = jnp.where(sub_pos >= d, Gc - gpad_ref[lo:lo + L, :], NEG)
                p = qc * kpad_ref[lo:lo + L, :] * jnp.exp(arg)
            ps.append(p)
        abc = _dot(jnp.concatenate(ps, axis=0).astype(BF16), ind_ref[...])
        for d in range(GLA_SUB):
            lo = PAD + c * L - d
            o_c = o_c + abc[d * L:(d + 1) * L] * vpad_ref[lo:lo + L, :]
        outs.append(o_c)

        st_ref[...] = (jnp.exp(GL[c * L:c * L + 1, :]) * st
                       + _dot_tn(vc16, k_dec[sl].astype(BF16)))

    o = jnp.concatenate(outs, axis=0)
    o2 = o * o
    hi = o2.astype(BF16)
    lo = (o2 - hi.astype(F32)).astype(BF16)
    ms = _dot(hi, indm_ref[...]) + _dot(lo, indm_ref[...])
    o_ref[...] = (o * lax.rsqrt(ms + EPS) * on_ref[...] * _silu(r)).astype(o_ref.dtype)


def _gla(gla, small, w2p, b2, onorm, b, s):
    ts = GLA_ROWS
    ns = s // ts
    mbig, ind, indm = _gla_consts(CUMSUM_ROWS)
    row = lambda bi, i: (bi * ns + i, 0)
    full = lambda bi, i: (0, 0)
    return pl.pallas_call(
        _gla_body,
        grid=(b, ns),
        in_specs=[pl.BlockSpec((ts, 768), row),
                  pl.BlockSpec((ts, LANES), row),
                  pl.BlockSpec((LANES, 128), full),
                  pl.BlockSpec((1, 128), full),
                  pl.BlockSpec((1, GROUP_WIDTH), full),
                  pl.BlockSpec(mbig.shape, full),
                  pl.BlockSpec((128, GROUP_WIDTH), full),
                  pl.BlockSpec((GROUP_WIDTH, GROUP_WIDTH), full)],
        out_specs=pl.BlockSpec((ts, GROUP_WIDTH), row),
        out_shape=jax.ShapeDtypeStruct((b * s, GROUP_WIDTH), BF16),
        scratch_shapes=[pltpu.VMEM((GROUP_WIDTH, 128), F32),
                        pltpu.VMEM((GLA_SUB + ts, 128), F32),
                        pltpu.VMEM((GLA_SUB + ts, 128), F32),
                        pltpu.VMEM((GLA_SUB + ts, GROUP_WIDTH), F32)],
        compiler_params=_cparams(("parallel", "arbitrary")),
        name="gla",
    )(gla, small, w2p, b2, onorm, mbig, ind, indm)


def _head_cols(src, lane0, rows, width, nheads):
    lane_h = lax.broadcasted_iota(jnp.int32, (rows, nheads * width), 1) // width
    out = jnp.broadcast_to(src[:, lane0:lane0 + 1], (rows, nheads * width))
    for h in range(1, nheads):
        out = jnp.where(lane_h == h, jnp.broadcast_to(src[:, lane0 + h:lane0 + h + 1],
                                                      (rows, nheads * width)), out)
    return out


def _ssd_body(ssm_ref, small_ref, cw_ref, cb_ref, dtb_ref, alog_ref, dskip_ref, nw_ref, tri_ref,
              o_ref, prev_ref, st_ref):
    ts = ssm_ref.shape[0]
    L = SSM_CHUNK

    @pl.when(pl.program_id(1) == 0)
    def _():
        prev_ref[...] = jnp.zeros_like(prev_ref)
        st_ref[...] = jnp.zeros_like(st_ref)

    z = ssm_ref[:, 0:GROUP_WIDTH]
    xbc = ssm_ref[:, GROUP_WIDTH:GROUP_WIDTH + SSM_CONV_DIM]
    prev8 = prev_ref[...]
    row8 = lax.broadcasted_iota(jnp.int32, (8, SSM_CONV_DIM), 0)
    conv = cb_ref[...] + cw_ref[SSM_CONV - 1:SSM_CONV, :] * xbc
    for sft in range(1, SSM_CONV):
        rolled = pltpu.roll(xbc, sft, 0)
        top = jnp.where(row8 < sft, pltpu.roll(prev8, sft, 0), rolled[0:8])
        shifted = jnp.concatenate([top, rolled[8:]], axis=0)
        conv = conv + cw_ref[SSM_CONV - 1 - sft:SSM_CONV - sft, :] * shifted
    prev_ref[...] = xbc[ts - 8:ts]
    xc = _silu(conv)
    xs = xc[:, 0:GROUP_WIDTH]
    bm = xc[:, GROUP_WIDTH:GROUP_WIDTH + 2 * SSM_STATE]
    cm = xc[:, GROUP_WIDTH + 2 * SSM_STATE:]

    dt = _softplus(small_ref[...] + dtb_ref[...])
    a = -jnp.exp(alog_ref[...]) * dt
    xdt = xs * _head_cols(dt, SMALL_DT, ts, HEAD_DIM, HEADS)

    ii = lax.broadcasted_iota(jnp.int32, (L, L), 0)
    jj = lax.broadcasted_iota(jnp.int32, (L, L), 1)
    lane_lo = jj < HEAD_DIM
    row_lo = ii < HEAD_DIM
    tri = tri_ref[...]
    ys = []
    for c in range(ts // L):
        sl = slice(c * L, (c + 1) * L)
        cs = _exact_lhs_dot(tri, a[sl], wide=False)
        cs_t = cs.T
        cs_last = cs[L - 1:L, :]
        y_groups = []
        for grp in range(2):
            h0 = 2 * grp
            b_g = bm[sl, grp * SSM_STATE:(grp + 1) * SSM_STATE].astype(BF16)
            c_g = cm[sl, grp * SSM_STATE:(grp + 1) * SSM_STATE].astype(BF16)
            x_g = xdt[sl, grp * LANES:(grp + 1) * LANES]
            cb = _dot_nt(c_g, b_g)
            ws = []
            for h in (h0, h0 + 1):
                colh = cs[:, SMALL_DT + h:SMALL_DT + h + 1]
                rowh = cs_t[SMALL_DT + h:SMALL_DT + h + 1, :]
                ws.append(cb * jnp.exp(jnp.where(jj <= ii, colh - rowh, NEG)))
            yd = _dot(jnp.concatenate(ws, axis=0).astype(BF16), x_g.astype(BF16))
            y_diag = jnp.where(lane_lo, yd[0:L], yd[L:2 * L])
            cs_g = _head_cols(cs, SMALL_DT + h0, L, HEAD_DIM, 2)
            st_g = st_ref[grp * LANES:(grp + 1) * LANES, :]
            y_off = _dot_nt(c_g, st_g.astype(BF16)) * jnp.exp(cs_g)
            y_groups.append(y_diag + y_off)
            last_g = _head_cols(cs_last, SMALL_DT + h0, 1, HEAD_DIM, 2)
            x_dec = x_g * jnp.exp(last_g - cs_g)
            new = _dot_tn(x_dec.astype(BF16), b_g)
            e0 = jnp.exp(cs_last[:, SMALL_DT + h0:SMALL_DT + h0 + 1])
            e1 = jnp.exp(cs_last[:, SMALL_DT + h0 + 1:SMALL_DT + h0 + 2])
            st_ref[grp * LANES:(grp + 1) * LANES, :] = jnp.where(row_lo, e0, e1) * st_g + new
        ys.append(jnp.concatenate(y_groups, axis=1))
    y = jnp.concatenate(ys, axis=0)
    y = (y + dskip_ref[...] * xs) * _silu(z)
    halves = []
    for grp in range(2):
        yg = y[:, grp * LANES:(grp + 1) * LANES]
        halves.append(_rms(yg, nw_ref[:, grp * LANES:(grp + 1) * LANES]))
    o_ref[...] = jnp.concatenate(halves, axis=1).astype(o_ref.dtype)


def _ssd(ssm, small, cw, cb, dtb, alog, dskip, nw, b, s):
    ts = SCAN_ROWS
    ns = s // ts
    tri = jnp.asarray(np.tril(np.ones((SSM_CHUNK, SSM_CHUNK), np.float32)), BF16)
    row = lambda bi, i: (bi * ns + i, 0)
    full = lambda bi, i: (0, 0)
    return pl.pallas_call(
        _ssd_body,
        grid=(b, ns),
        in_specs=[pl.BlockSpec((ts, 1024), row),
                  pl.BlockSpec((ts, LANES), row),
                  pl.BlockSpec((SSM_CONV, SSM_CONV_DIM), full),
                  pl.BlockSpec((1, SSM_CONV_DIM), full),
                  pl.BlockSpec((1, LANES), full),
                  pl.BlockSpec((1, LANES), full),
                  pl.BlockSpec((1, GROUP_WIDTH), full),
                  pl.BlockSpec((1, GROUP_WIDTH), full),
                  pl.BlockSpec((SSM_CHUNK, SSM_CHUNK), full)],
        out_specs=pl.BlockSpec((ts, GROUP_WIDTH), row),
        out_shape=jax.ShapeDtypeStruct((b * s, GROUP_WIDTH), BF16),
        scratch_shapes=[pltpu.VMEM((8, SSM_CONV_DIM), F32), pltpu.VMEM((GROUP_WIDTH, SSM_STATE), F32)],
        compiler_params=_cparams(("parallel", "arbitrary")),
        name="ssd",
    )(ssm, small, cw, cb, dtb, alog, dskip, nw, tri)


def _out_ffn_body(x_ref, yaT_ref, yb_ref, ycT_ref, yd_ref, wo_ref, n2_ref, wg_ref, wu_ref, wd_ref,
                  fn_ref, o_ref, *, final_norm):
    x1 = (x_ref[...]
          + _dot_tn(yaT_ref[0], wo_ref[0]) + _dot(yb_ref[...], wo_ref[1])
          + _dot_tn(ycT_ref[0], wo_ref[2]) + _dot(yd_ref[...], wo_ref[3]))
    h2 = _rms(x1, n2_ref[...]).astype(BF16)
    hidden = _silu(_dot(h2, wg_ref[...])) * _dot(h2, wu_ref[...])
    out = x1 + _dot(hidden.astype(BF16), wd_ref[...])
    if final_norm:
        out = _rms(out, fn_ref[...])
    o_ref[...] = out


def _out_ffn(x2, yaT, yb, ycT, yd, wo, n2, wg, wu, wd, fn, b, s, final_norm):
    tm = TILE_ROWS
    ns = s // tm
    row = lambda i: (i, 0)
    tr = lambda i: (i // ns, 0, i % ns)

    def resident(shape):
        return pl.BlockSpec(shape, lambda i: (0,) * len(shape), pipeline_mode=pl.Buffered(1))

    return pl.pallas_call(
        functools.partial(_out_ffn_body, final_norm=final_norm),
        grid=(b * ns,),
        in_specs=[pl.BlockSpec((tm, D_MODEL), row),
                  pl.BlockSpec((1, GROUP_WIDTH, tm), tr),
                  pl.BlockSpec((tm, GROUP_WIDTH), row),
                  pl.BlockSpec((1, GROUP_WIDTH, tm), tr),
                  pl.BlockSpec((tm, GROUP_WIDTH), row),
                  resident((4, GROUP_WIDTH, D_MODEL)),
                  resident((1, D_MODEL)),
                  resident((D_MODEL, FFN_HIDDEN)),
                  resident((D_MODEL, FFN_HIDDEN)),
                  resident((FFN_HIDDEN, D_MODEL)),
                  resident((1, D_MODEL))],
        out_specs=pl.BlockSpec((tm, D_MODEL), row),
        out_shape=jax.ShapeDtypeStruct((b * s, D_MODEL), F32),
        compiler_params=_cparams(("parallel",)),
        name="out_ffn",
    )(x2, yaT, yb, ycT, yd, wo, n2, wg, wu, wd, fn)


def _prep_w_in(w):
    z = lambda n: jnp.zeros((w.shape[0], n), w.dtype)
    small = jnp.concatenate([w[:, 768:772], w[:, 2996:3000], z(8), w[:, 1540:1556], z(32),
                             w[:, 1940:1972], z(32)], axis=1)
    out = jnp.concatenate([w[:, 0:768], w[:, 772:1540], w[:, 1556:1940], w[:, 1972:2996], small], axis=1)
    return out.astype(BF16)


def _lane_row(vals, lane0):
    return jnp.zeros((1, LANES), F32).at[0, lane0:lane0 + vals.shape[0]].set(vals.astype(F32))


def _prep_mla_w(w_uq, w_ukv):
    wq = w_uq.reshape(MLA_Q_LORA, HEADS, HEAD_DIM + MLA_ROPE)
    wq = jnp.pad(wq, ((0, 0), (0, 0), (0, LANES - HEAD_DIM - MLA_ROPE))).reshape(MLA_Q_LORA, HEADS * LANES)
    wkv = w_ukv.reshape(MLA_KV_LORA, HEADS, 2 * HEAD_DIM)
    wk = jnp.pad(wkv[:, :, :HEAD_DIM], ((0, 0), (0, 0), (0, LANES - HEAD_DIM))).reshape(MLA_KV_LORA, HEADS * LANES)
    wv = wkv[:, :, HEAD_DIM:].reshape(MLA_KV_LORA, GROUP_WIDTH)
    return wq.astype(BF16), wk.astype(BF16), wv.astype(BF16)


def kernel(x, positions, norm1, w_in, fox_f_bias, gla_gate_w2, gla_gate_b, gla_out_norm, mla_q_norm,
           mla_w_uq, mla_kv_norm, mla_w_ukv, ssm_conv_w, ssm_conv_b, ssm_dt_bias, ssm_A_log, ssm_D,
           ssm_norm, w_out, norm2, w_gate, w_up, w_down, final_norm):
    b, s, d = x.shape
    depth = w_in.shape[0]
    assert d == D_MODEL and s % max(ATT_TQ, TILE_ROWS) == 0 and ATT_TQ % ATT_TK == 0
    t = b * s
    x2 = x.reshape(t, d)

    half = MLA_ROPE // 2
    inv = ROPE_THETA ** (-jnp.arange(half, dtype=F32) / half)
    invf = jnp.zeros((1, LANES), F32).at[0, 64:64 + MLA_ROPE].set(jnp.concatenate([inv, inv]))
    cos_t, sin_t = _rope_tables(positions.reshape(t, 1), invf)

    tri_att = jnp.asarray(np.tril(np.ones((CUMSUM_ROWS, CUMSUM_ROWS), np.float32)), BF16)

    for l in range(depth):
        foxq, foxkv, gla, mla, ssm, small = _inproj(x2, norm1[l][None, :], _prep_w_in(w_in[l]))

        qT, kA, vT = _fox_prep(foxq, foxkv, small, _lane_row(fox_f_bias[l], SMALL_FOX_F), tri_att, b, s)
        ya = _flash(qT, kA, vT).reshape(b, GROUP_WIDTH, s)

        wq, wk, wv = _prep_mla_w(mla_w_uq[l], mla_w_ukv[l])
        qT, kA, vT = _mla_prep(mla, small, cos_t, sin_t, mla_q_norm[l][None, :], mla_kv_norm[l][None, :],
                               wq, wk, wv, b, s)
        yc = _flash(qT, kA, vT).reshape(b, GROUP_WIDTH, s)

        w2p = jnp.zeros((LANES, HEADS * GLA_DK), F32).at[SMALL_GATE:SMALL_GATE + GLA_GATE_RANK].set(
            gla_gate_w2[l]).astype(BF16)
        yb = _gla(gla, small, w2p, gla_gate_b[l][None, :].astype(F32),
                  jnp.tile(gla_out_norm[l], HEADS)[None, :].astype(F32), b, s)

        yd = _ssd(ssm, small, ssm_conv_w[l], ssm_conv_b[l][None, :],
                  _lane_row(ssm_dt_bias[l], SMALL_DT), _lane_row(ssm_A_log[l], SMALL_DT),
                  jnp.repeat(ssm_D[l], HEAD_DIM)[None, :].astype(F32), ssm_norm[l][None, :], b, s)

        x2 = _out_ffn(x2, ya, yb, yc, yd, w_out[l].reshape(4, GROUP_WIDTH, D_MODEL).astype(BF16),
                      norm2[l][None, :], w_gate[l].astype(BF16), w_up[l].astype(BF16),
                      w_down[l].astype(BF16), final_norm[None, :], b, s,
                      final_norm=(l == depth - 1))
    return x2.reshape(b, s, d)
```

```python
import functools
import math

import numpy as np
import jax
import jax.numpy as jnp
from jax import lax
from jax.experimental import pallas as pl
from jax.experimental.pallas import tpu as pltpu

F32 = jnp.float32
BF16 = jnp.bfloat16

D_MODEL = 1024
GROUP_WIDTH = 256
HEADS = 4
HEAD_DIM = 64
GLA_DK = 32
GLA_GATE_RANK = 16
GLA_GATE_TAU = 16.0
GLA_CHUNK = 64
GLA_SUB = 16
MLA_ROPE = 32
MLA_Q_LORA = 256
MLA_KV_LORA = 128
ROPE_THETA = 10000.0
SSM_STATE = 128
SSM_CONV = 4
SSM_CHUNK = 128
SSM_CONV_DIM = 768
FFN_HIDDEN = 2816
EPS = 1e-6
NEG = -1e30
LOG2E = math.log2(math.e)
SUM_ROWS = 16

LANES = 128
QK_PAD = 128
VMEM_LIMIT = 56 * 1024 * 1024

SEG_FOXQ = (0, 256)
SEG_FOXKV = (256, 768)
SEG_GLA = (768, 1536)
SEG_MLA = (1536, 1920)
SEG_SSM = (1920, 2944)
SEG_SMALL = (2944, 3072)
IN_PAD = 3072
SMALL_FOX_F = 0
SMALL_DT = 4
SMALL_GATE = 16
SMALL_KROPE = 64

TILE_ROWS = 512
ATT_TQ = 512
ATT_TK = 256
FLASH_UNROLL = 4
SCAN_ROWS = 512
GLA_ROWS = 512
CUMSUM_ROWS = 256


def _cparams(sem):
    return pltpu.CompilerParams(dimension_semantics=sem, vmem_limit_bytes=VMEM_LIMIT)


def _rms(x, g):
    ms = jnp.mean(x * x, axis=-1, keepdims=True)
    return x * lax.rsqrt(ms + EPS) * g


def _log_sigmoid(x):
    return jnp.minimum(x, 0.0) - jnp.log1p(jnp.exp(-jnp.abs(x)))


def _softplus(x):
    return jnp.maximum(x, 0.0) + jnp.log1p(jnp.exp(-jnp.abs(x)))


def _silu(x):
    return x / (1.0 + jnp.exp(-x))


def _split3(x):
    hi = x.astype(BF16)
    r = x - hi.astype(F32)
    mid = r.astype(BF16)
    lo = (r - mid.astype(F32)).astype(BF16)
    return hi, mid, lo


def _dot(a, b):
    return jnp.dot(a, b, preferred_element_type=F32)


def _dot_nt(a, b):
    return lax.dot_general(a, b, (((1,), (1,)), ((), ())), preferred_element_type=F32)


def _dot_tn(a, b):
    return lax.dot_general(a, b, (((0,), (0,)), ((), ())), preferred_element_type=F32)


def _exact_lhs_dot(m01, x, wide=True):
    hi, mid, lo = _split3(x)
    if not wide:
        return _dot(m01, hi) + _dot(m01, mid) + _dot(m01, lo)
    w = x.shape[1]
    y = _dot(m01, jnp.concatenate([hi, mid, lo], axis=1))
    return y[:, 0:w] + y[:, w:2 * w] + y[:, 2 * w:3 * w]


def _inproj_body(x_ref, g_ref, w_ref, foxq_ref, foxkv_ref, gla_ref, mla_ref, ssm_ref, small_ref):
    h = _rms(x_ref[...], g_ref[...]).astype(BF16)

    def seg(s):
        return _dot(h, w_ref[:, s[0]:s[1]])

    foxq_ref[...] = seg(SEG_FOXQ)
    foxkv_ref[...] = seg(SEG_FOXKV).astype(BF16)
    gla_ref[...] = seg(SEG_GLA)
    mla_ref[...] = seg(SEG_MLA)
    ssm_ref[...] = seg(SEG_SSM)
    small_ref[...] = seg(SEG_SMALL)


def _inproj(x2, g, w):
    t = x2.shape[0]
    tm = TILE_ROWS
    widths = [s[1] - s[0] for s in (SEG_FOXQ, SEG_FOXKV, SEG_GLA, SEG_MLA, SEG_SSM, SEG_SMALL)]
    dts = [F32, BF16, F32, F32, F32, F32]
    return pl.pallas_call(
        _inproj_body,
        grid=(t // tm,),
        in_specs=[pl.BlockSpec((tm, D_MODEL), lambda i: (i, 0)),
                  pl.BlockSpec((1, D_MODEL), lambda i: (0, 0)),
                  pl.BlockSpec((D_MODEL, IN_PAD), lambda i: (0, 0))],
        out_specs=[pl.BlockSpec((tm, w_), lambda i: (i, 0)) for w_ in widths],
        out_shape=[jax.ShapeDtypeStruct((t, w_), d_) for w_, d_ in zip(widths, dts)],
        compiler_params=_cparams(("parallel",)),
        name="inproj",
    )(x2, g, w)


def _rope_table_body(pos_ref, invf_ref, cos_ref, sin_ref):
    ang = pos_ref[...].astype(F32) * invf_ref[...]
    cos_ref[...] = jnp.cos(ang)
    sin_ref[...] = jnp.sin(ang)


def _rope_tables(pos_col, invf):
    t = pos_col.shape[0]
    tm = TILE_ROWS
    return pl.pallas_call(
        _rope_table_body,
        grid=(t // tm,),
        in_specs=[pl.BlockSpec((tm, 1), lambda i: (i, 0)),
                  pl.BlockSpec((1, LANES), lambda i: (0, 0))],
        out_specs=[pl.BlockSpec((tm, LANES), lambda i: (i, 0))] * 2,
        out_shape=[jax.ShapeDtypeStruct((t, LANES), F32)] * 2,
        compiler_params=_cparams(("parallel",)),
        name="rope_tables",
    )(pos_col, invf)


def _store_vT(vT_ref, v_all):
    for blk in range(v_all.shape[0] // ATT_TK):
        for pair in range(2):
            vp = v_all[blk * ATT_TK:(blk + 1) * ATT_TK, pair * LANES:(pair + 1) * LANES]
            vT_ref[0, 0, blk, pair * LANES:(pair + 1) * LANES, :] = vp.T.astype(BF16)


def _fox_prep_body(foxq_ref, foxkv_ref, small_ref, fb_ref, tri_ref, qT_ref, kA_ref, vT_ref, carry_ref):
    ts = foxq_ref.shape[0]

    @pl.when(pl.program_id(1) == 0)
    def _():
        carry_ref[...] = jnp.zeros_like(carry_ref)

    logf = _log_sigmoid(small_ref[...] + fb_ref[...])
    blk = tri_ref.shape[0]
    pieces, run = [], carry_ref[...]
    for i0 in range(0, ts, blk):
        part = _exact_lhs_dot(tri_ref[...], logf[i0:i0 + blk]) + run
        run = part[blk - 1:blk, :]
        pieces.append(part)
    fcum = jnp.concatenate(pieces, axis=0)
    carry_ref[...] = run
    f_hi, f_mid, f_lo = [p.astype(F32) for p in _split3(fcum * LOG2E)]

    lane = lax.broadcasted_iota(jnp.int32, (ts, LANES), 1)
    low = lane < HEAD_DIM
    _store_vT(vT_ref, foxkv_ref[:, 256:512].astype(F32))
    for pair in range(2):
        qp = foxq_ref[:, pair * LANES:(pair + 1) * LANES] * (LOG2E * HEAD_DIM ** -0.5)
        kp = foxkv_ref[:, pair * LANES:(pair + 1) * LANES].astype(F32)
        for e in range(2):
            h = 2 * pair + e
            qh = qp if e == 0 else pltpu.roll(qp, HEAD_DIM, 1)
            kh = kp if e == 0 else pltpu.roll(kp, HEAD_DIM, 1)

            def col(src, dst):
                return pltpu.roll(src, (dst - h) % LANES, 1)

            qa = jnp.where(low, qh,
                 jnp.where(lane == 64, col(f_hi, 64),
                 jnp.where(lane == 65, col(f_mid, 65),
                 jnp.where(lane == 66, col(f_lo, 66),
                 jnp.where(lane < 70, 1.0, 0.0)))))
            ka = jnp.where(low, kh,
                 jnp.where(lane < 67, 1.0,
                 jnp.where(lane == 67, -col(f_hi, 67),
                 jnp.where(lane == 68, -col(f_mid, 68),
                 jnp.where(lane == 69, -col(f_lo, 69), 0.0)))))
            qT_ref[0, h, 0] = qa.T.astype(BF16)
            kA_ref[0, h] = ka.astype(BF16)


def _att_out_shapes(b, s, ts):
    return [jax.ShapeDtypeStruct((b, HEADS, s // ts, QK_PAD, ts), BF16),
            jax.ShapeDtypeStruct((b, HEADS, s, QK_PAD), BF16),
            jax.ShapeDtypeStruct((b, s // ts, ts // ATT_TK, GROUP_WIDTH, ATT_TK), BF16)]


def _att_out_specs(ts):
    return [pl.BlockSpec((1, HEADS, 1, QK_PAD, ts), lambda b, i: (b, 0, i, 0, 0)),
            pl.BlockSpec((1, HEADS, ts, QK_PAD), lambda b, i: (b, 0, i, 0)),
            pl.BlockSpec((1, 1, ts // ATT_TK, GROUP_WIDTH, ATT_TK), lambda b, i: (b, i, 0, 0, 0))]


def _fox_prep(foxq, foxkv, small, fbias, tri, b, s):
    ts = TILE_ROWS
    ns = s // ts
    return pl.pallas_call(
        _fox_prep_body,
        grid=(b, ns),
        in_specs=[pl.BlockSpec((ts, 256), lambda bi, i: (bi * ns + i, 0)),
                  pl.BlockSpec((ts, 512), lambda bi, i: (bi * ns + i, 0)),
                  pl.BlockSpec((ts, LANES), lambda bi, i: (bi * ns + i, 0)),
                  pl.BlockSpec((1, LANES), lambda bi, i: (0, 0)),
                  pl.BlockSpec(tri.shape, lambda bi, i: (0, 0))],
        out_specs=_att_out_specs(ts),
        out_shape=_att_out_shapes(b, s, ts),
        scratch_shapes=[pltpu.VMEM((1, LANES), F32)],
        compiler_params=_cparams(("parallel", "arbitrary")),
        name="fox_prep",
    )(foxq, foxkv, small, fbias, tri)


def _mla_prep_body(mla_ref, small_ref, cos_ref, sin_ref, qn_ref, kvn_ref, wq_ref, wk_ref, wv_ref,
                   qT_ref, kA_ref, vT_ref):
    ts = mla_ref.shape[0]
    lane = lax.broadcasted_iota(jnp.int32, (ts, LANES), 1)
    cosv = cos_ref[...]
    sinv = sin_ref[...]
    half = MLA_ROPE // 2
    sin_a = jnp.where((lane >= 64) & (lane < 64 + half), -sinv, 0.0)
    sin_b = jnp.where((lane >= 64 + half) & (lane < 64 + MLA_ROPE), sinv, 0.0)

    def rope(x):
        return (x * cosv + pltpu.roll(x, LANES - half, 1) * sin_a + pltpu.roll(x, half, 1) * sin_b)

    cq = _rms(mla_ref[:, 0:MLA_Q_LORA], qn_ref[...]).astype(BF16)
    ckv = _rms(mla_ref[:, MLA_Q_LORA:MLA_Q_LORA + MLA_KV_LORA], kvn_ref[...]).astype(BF16)
    kr = rope(jnp.where((lane >= 64) & (lane < 64 + MLA_ROPE), small_ref[...], 0.0))
    q_all = _dot(cq, wq_ref[...]) * (LOG2E * (HEAD_DIM + MLA_ROPE) ** -0.5)
    k_all = _dot(ckv, wk_ref[...])
    for h in range(HEADS):
        qh = rope(q_all[:, h * LANES:(h + 1) * LANES])
        qT_ref[0, h, 0] = qh.T.astype(BF16)
        kA_ref[0, h] = (k_all[:, h * LANES:(h + 1) * LANES] + kr).astype(BF16)
    _store_vT(vT_ref, _dot(ckv, wv_ref[...]))


def _mla_prep(mla, small, cos_t, sin_t, qn, kvn, wq, wk, wv, b, s):
    ts = TILE_ROWS
    ns = s // ts
    row = lambda bi, i: (bi * ns + i, 0)
    full = lambda bi, i: (0, 0)
    return pl.pallas_call(
        _mla_prep_body,
        grid=(b, ns),
        in_specs=[pl.BlockSpec((ts, 384), row),
                  pl.BlockSpec((ts, LANES), row),
                  pl.BlockSpec((ts, LANES), row),
                  pl.BlockSpec((ts, LANES), row),
                  pl.BlockSpec((1, MLA_Q_LORA), full),
                  pl.BlockSpec((1, MLA_KV_LORA), full),
                  pl.BlockSpec((MLA_Q_LORA, HEADS * LANES), full),
                  pl.BlockSpec((MLA_KV_LORA, HEADS * LANES), full),
                  pl.BlockSpec((MLA_KV_LORA, GROUP_WIDTH), full)],
        out_specs=_att_out_specs(ts),
        out_shape=_att_out_shapes(b, s, ts),
        compiler_params=_cparams(("parallel", "parallel")),
        name="mla_prep",
    )(mla, small, cos_t, sin_t, qn, kvn, wq, wk, wv)


def _flash_body(qT_ref, k_ref, vT_ref, o_ref, m_ref, acc_ref, s_ref, p_ref, a_ref, mb_ref, *, tq, tk):
    nq = qT_ref.shape[2]
    nb = tq // tk
    assert nb % 2 == 0
    ones = jnp.ones((SUM_ROWS, tk), BF16)

    def score_block(qi, j, slot, lo=0, with_max=True):
        start = pl.multiple_of(j * tk, tk)
        s = _dot(k_ref[0, 0, pl.ds(start, tk), :], qT_ref[0, 0, qi, :, lo:])
        s_ref[slot, :, lo:] = s
        if with_max:
            mb_ref[slot] = jnp.max(s, axis=0, keepdims=True)

    score_block(0, 0, 0)

    def q_tile(i, carry):
        m_ref[...] = jnp.full_like(m_ref, NEG)
        acc_ref[...] = jnp.zeros_like(acc_ref)
        p_ref[1] = jnp.zeros_like(p_ref[1])
        a_ref[1] = jnp.ones_like(a_ref[1])
        n_full = nb * i

        def scores(j, slot, lo=0, with_max=True):
            score_block(i, j, slot, lo, with_max)

        def softmax(slot, lo=0, diag=False):
            s = s_ref[slot, :, lo:]
            if diag:
                keep = (lax.broadcasted_iota(jnp.int32, (tk, tk), 0)
                        <= lax.broadcasted_iota(jnp.int32, (tk, tk), 1))
                left = jnp.where(keep, s[:, :tk], NEG)
                s = left if s.shape[1] == tk else jnp.concatenate([left, s[:, tk:]], axis=1)
                m_blk = jnp.max(s, axis=0, keepdims=True)
            else:
                m_blk = mb_ref[slot]
            m_prev = m_ref[:, lo:]
            m_new = jnp.maximum(m_prev, m_blk)
            a_ref[slot, :, lo:] = jnp.exp2(m_prev - m_new)
            p_ref[slot, :, lo:] = jnp.exp2(s - m_new).astype(BF16)
            m_ref[:, lo:] = m_new

        def pv(j, slot, lo=0):
            v_ext = jnp.concatenate([vT_ref[0, j, 0], ones], axis=0)
            acc_ref[:, lo:] = (a_ref[slot, :, lo:] * acc_ref[:, lo:]
                               + _dot(v_ext, p_ref[slot, :, lo:]))

        def run_blocks(j, count):
            for u in range(count):
                scores(j + u + 1, (u + 1) % 2)
                softmax(u % 2)
                pv(jnp.maximum(j + u - 1, 0), (u + 1) % 2)

        per_trip = nb * FLASH_UNROLL
        n_trips = i // FLASH_UNROLL

        def body(t, c2):
            run_blocks(per_trip * t, per_trip)
            return c2

        lax.fori_loop(0, n_trips, body, 0)
        for r in range(1, FLASH_UNROLL):
            @pl.when(i % FLASH_UNROLL >= r)
            def _():
                run_blocks(per_trip * n_trips + nb * (r - 1), nb)
        for u in range(nb):
            if u + 1 < nb:
                scores(n_full + u + 1, (u + 1) % 2, lo=(u + 1) * tk, with_max=False)
            softmax(u % 2, lo=u * tk, diag=True)
            pv(jnp.maximum(n_full + u - 1, 0), (u + 1) % 2, lo=max(u - 1, 0) * tk)
        score_block(jnp.minimum(i + 1, nq - 1), 0, 0)
        pv(n_full + nb - 1, (nb - 1) % 2, lo=(nb - 1) * tk)
        o_ref[0, 0, i] = (acc_ref[0:HEAD_DIM, :] / acc_ref[HEAD_DIM:HEAD_DIM + 1, :]).astype(o_ref.dtype)
        return carry

    lax.fori_loop(0, nq, q_tile, 0)


def _flash(qT, kA, vT):
    b, _, nq, _, tq = qT.shape
    tk = ATT_TK
    s = nq * tq
    nk = s // tk
    vT5 = vT.reshape(b, nk, HEADS, HEAD_DIM, tk)
    return pl.pallas_call(
        functools.partial(_flash_body, tq=tq, tk=tk),
        grid=(b, HEADS),
        in_specs=[pl.BlockSpec((1, 1, nq, QK_PAD, tq), lambda bi, h: (bi, h, 0, 0, 0)),
                  pl.BlockSpec((1, 1, s, QK_PAD), lambda bi, h: (bi, h, 0, 0)),
                  pl.BlockSpec((1, nk, 1, HEAD_DIM, tk), lambda bi, h: (bi, 0, h, 0, 0))],
        out_specs=pl.BlockSpec((1, 1, nq, HEAD_DIM, tq), lambda bi, h: (bi, h, 0, 0, 0)),
        out_shape=jax.ShapeDtypeStruct((b, HEADS, nq, HEAD_DIM, tq), BF16),
        scratch_shapes=[pltpu.VMEM((1, tq), F32),
                        pltpu.VMEM((HEAD_DIM + SUM_ROWS, tq), F32),
                        pltpu.VMEM((2, tk, tq), F32),
                        pltpu.VMEM((2, tk, tq), BF16),
                        pltpu.VMEM((2, 1, tq), F32),
                        pltpu.VMEM((2, 1, tq), F32)],
        compiler_params=_cparams(("parallel", "parallel")),
        name="flash",
    )(qT, kA, vT5)


def _gla_consts(ts):
    idx = np.arange(ts)
    same = (idx[:, None] // GLA_CHUNK) == (idx[None, :] // GLA_CHUNK)
    mbig = (same & (idx[None, :] <= idx[:, None])).astype(np.float32)
    hd = np.arange(HEADS * GLA_DK) // GLA_DK
    hv = np.arange(GROUP_WIDTH) // HEAD_DIM
    ind_dk_dv = (hd[:, None] == hv[None, :]).astype(np.float32)
    ind_mean = (hv[:, None] == hv[None, :]).astype(np.float32) / HEAD_DIM
    return (jnp.asarray(mbig, BF16), jnp.asarray(ind_dk_dv, BF16), jnp.asarray(ind_mean, BF16))


def _pick_heads(stacked, rows):
    lane_h = lax.broadcasted_iota(jnp.int32, (rows, GROUP_WIDTH), 1) // HEAD_DIM
    out = jnp.zeros((rows, GROUP_WIDTH), F32)
    for h in range(HEADS):
        out = jnp.where(lane_h == h, stacked[h * rows:(h + 1) * rows, :], out)
    return out


def _gla_body(gla_ref, small_ref, w2_ref, b2_ref, on_ref, mbig_ref, ind_ref, indm_ref, o_ref, st_ref,
              gpad_ref, kpad_ref, vpad_ref):
    ts = gla_ref.shape[0]
    L = GLA_CHUNK
    nsub = L // GLA_SUB
    PAD = GLA_SUB

    @pl.when(pl.program_id(1) == 0)
    def _():
        st_ref[...] = jnp.zeros_like(st_ref)

    q = gla_ref[:, 0:128] * (GLA_DK ** -0.5)
    k = gla_ref[:, 128:256]
    v = gla_ref[:, 256:512]
    r = gla_ref[:, 512:768]
    x = _dot(small_ref[...].astype(BF16), w2_ref[...]) + b2_ref[...]
    g = _log_sigmoid(x) * (1.0 / GLA_GATE_TAU)
    cb = mbig_ref.shape[0]
    G = jnp.concatenate([_exact_lhs_dot(mbig_ref[...], g[i0:i0 + cb]) for i0 in range(0, ts, cb)], axis=0)
    r_rows, gl_rows = [], []
    for c in range(ts // L):
        for I in range(nsub):
            at = c * L + I * GLA_SUB
            r_rows.append(jnp.zeros((GLA_SUB, 128), F32) if I == 0
                          else jnp.broadcast_to(G[at - 1:at, :], (GLA_SUB, 128)))
        gl_rows.append(jnp.broadcast_to(G[(c + 1) * L - 1:(c + 1) * L, :], (L, 128)))
    R = jnp.concatenate(r_rows, axis=0)
    GL = jnp.concatenate(gl_rows, axis=0)
    q_inter = q * jnp.exp(G)
    q_norm = q * jnp.exp(G - R)
    k_dec = k * jnp.exp(GL - G)
    for ref, val in ((gpad_ref, G), (kpad_ref, k), (vpad_ref, v)):
        ref[0:PAD, :] = jnp.zeros((PAD, val.shape[1]), F32)
        ref[PAD:PAD + ts, :] = val

    lane_h = lax.broadcasted_iota(jnp.int32, (L, 128), 1) // GLA_DK
    row = lax.broadcasted_iota(jnp.int32, (L, 128), 0)
    sub_pos = row % GLA_SUB
    sub_idx = row // GLA_SUB
    row_blk = lax.broadcasted_iota(jnp.int32, (L, L * 2), 0) // GLA_SUB
    zeros_v = jnp.zeros((L, GROUP_WIDTH), BF16)

    def stack_heads(a):
        return jnp.concatenate([jnp.where(lane_h == h, a, 0.0) for h in range(HEADS)], axis=0)

    outs = []
    for c in range(ts // L):
        sl = slice(c * L, (c + 1) * L)
        Gc, kc, vc, qc = G[sl], k[sl], v[sl], q[sl]
        vc16 = vc.astype(BF16)

        r_next = jnp.concatenate([R[c * L + GLA_SUB:(c + 1) * L], R[(c + 1) * L - GLA_SUB:(c + 1) * L]],
                                 axis=0)
        base = kc * jnp.exp(jnp.where(sub_idx < nsub - 1, r_next - Gc, NEG))
        kts = []
        kt = jnp.zeros_like(base)
        for I in range(1, nsub):
            if I > 1:
                r_i = R[c * L + I * GLA_SUB:c * L + I * GLA_SUB + 1, :]
                r_p = R[c * L + (I - 1) * GLA_SUB:c * L + (I - 1) * GLA_SUB + 1, :]
                kt = kt * jnp.exp(r_i - r_p)
            kt = jnp.where(sub_idx == I - 1, base, kt)
            kts += [kt, jnp.zeros_like(kt)]
        kstack = jnp.concatenate(kts, axis=0).astype(BF16)
        sc = _dot_nt(stack_heads(q_norm[sl]).astype(BF16), kstack)
        a_rows = []
        for h in range(HEADS):
            sh = sc[h * L:(h + 1) * L]
            a_h = jnp.zeros((L, 2 * L), F32)
            for I in range(1, nsub):
                a_h = jnp.where(row_blk == I, sh[:, (I - 1) * 2 * L:I * 2 * L], a_h)
            a_rows.append(a_h)
        a_st = jnp.concatenate(a_rows, axis=0).astype(BF16)
        v_pad = jnp.concatenate([vc16, zeros_v], axis=0)
        st = st_ref[...]
        big = _dot(a_st, v_pad) + _dot_nt(stack_heads(q_inter[sl]).astype(BF16), st.astype(BF16))
        o_c = _pick_heads(big, L)

        ps = []
        for d in range(GLA_SUB):
            if d == 0:
                p = qc * kc
            else:
                lo = PAD + c * L - d
                arg = jnp.where(sub_pos >= d, Gc - gpad_ref[lo:lo + L, :], NEG)
                p = qc * kpad_ref[lo:lo + L, :] * jnp.exp(arg)
            ps.append(p)
        abc = _dot(jnp.concatenate(ps, axis=0).astype(BF16), ind_ref[...])
        for d in range(GLA_SUB):
            lo = PAD + c * L - d
            o_c = o_c + abc[d * L:(d + 1) * L] * vpad_ref[lo:lo + L, :]
        outs.append(o_c)

        st_ref[...] = (jnp.exp(GL[c * L:c * L + 1, :]) * st
                       + _dot_tn(vc16, k_dec[sl].astype(BF16)))

    o = jnp.concatenate(outs, axis=0)
    o2 = o * o
    hi = o2.astype(BF16)
    lo = (o2 - hi.astype(F32)).astype(BF16)
    ms = _dot(hi, indm_ref[...]) + _dot(lo, indm_ref[...])
    o_ref[...] = (o * lax.rsqrt(ms + EPS) * on_ref[...] * _silu(r)).astype(o_ref.dtype)


def _gla(gla, small, w2p, b2, onorm, b, s):
    ts = GLA_ROWS
    ns = s // ts
    mbig, ind, indm = _gla_consts(CUMSUM_ROWS)
    row = lambda bi, i: (bi * ns + i, 0)
    full = lambda bi, i: (0, 0)
    return pl.pallas_call(
        _gla_body,
        grid=(b, ns),
        in_specs=[pl.BlockSpec((ts, 768), row),
                  pl.BlockSpec((ts, LANES), row),
                  pl.BlockSpec((LANES, 128), full),
                  pl.BlockSpec((1, 128), full),
                  pl.BlockSpec((1, GROUP_WIDTH), full),
                  pl.BlockSpec(mbig.shape, full),
                  pl.BlockSpec((128, GROUP_WIDTH), full),
                  pl.BlockSpec((GROUP_WIDTH, GROUP_WIDTH), full)],
        out_specs=pl.BlockSpec((ts, GROUP_WIDTH), row),
        out_shape=jax.ShapeDtypeStruct((b * s, GROUP_WIDTH), BF16),
        scratch_shapes=[pltpu.VMEM((GROUP_WIDTH, 128), F32),
                        pltpu.VMEM((GLA_SUB + ts, 128), F32),
                        pltpu.VMEM((GLA_SUB + ts, 128), F32),
                        pltpu.VMEM((GLA_SUB + ts, GROUP_WIDTH), F32)],
        compiler_params=_cparams(("parallel", "arbitrary")),
        name="gla",
    )(gla, small, w2p, b2, onorm, mbig, ind, indm)


def _head_cols(src, lane0, rows, width, nheads):
    lane_h = lax.broadcasted_iota(jnp.int32, (rows, nheads * width), 1) // width
    out = jnp.broadcast_to(src[:, lane0:lane0 + 1], (rows, nheads * width))
    for h in range(1, nheads):
        out = jnp.where(lane_h == h, jnp.broadcast_to(src[:, lane0 + h:lane0 + h + 1],
                                                      (rows, nheads * width)), out)
    return out


def _ssd_body(ssm_ref, small_ref, cw_ref, cb_ref, dtb_ref, alog_ref, dskip_ref, nw_ref, tri_ref,
              o_ref, prev_ref, st_ref):
    ts = ssm_ref.shape[0]
    L = SSM_CHUNK

    @pl.when(pl.program_id(1) == 0)
    def _():
        prev_ref[...] = jnp.zeros_like(prev_ref)
        st_ref[...] = jnp.zeros_like(st_ref)

    z = ssm_ref[:, 0:GROUP_WIDTH]
    xbc = ssm_ref[:, GROUP_WIDTH:GROUP_WIDTH + SSM_CONV_DIM]
    prev8 = prev_ref[...]
    row8 = lax.broadcasted_iota(jnp.int32, (8, SSM_CONV_DIM), 0)
    conv = cb_ref[...] + cw_ref[SSM_CONV - 1:SSM_CONV, :] * xbc
    for sft in range(1, SSM_CONV):
        rolled = pltpu.roll(xbc, sft, 0)
        top = jnp.where(row8 < sft, pltpu.roll(prev8, sft, 0), rolled[0:8])
        shifted = jnp.concatenate([top, rolled[8:]], axis=0)
        conv = conv + cw_ref[SSM_CONV - 1 - sft:SSM_CONV - sft, :] * shifted
    prev_ref[...] = xbc[ts - 8:ts]
    xc = _silu(conv)
    xs = xc[:, 0:GROUP_WIDTH]
    bm = xc[:, GROUP_WIDTH:GROUP_WIDTH + 2 * SSM_STATE]
    cm = xc[:, GROUP_WIDTH + 2 * SSM_STATE:]

    dt = _softplus(small_ref[...] + dtb_ref[...])
    a = -jnp.exp(alog_ref[...]) * dt
    xdt = xs * _head_cols(dt, SMALL_DT, ts, HEAD_DIM, HEADS)

    ii = lax.broadcasted_iota(jnp.int32, (L, L), 0)
    jj = lax.broadcasted_iota(jnp.int32, (L, L), 1)
    lane_lo = jj < HEAD_DIM
    row_lo = ii < HEAD_DIM
    tri = tri_ref[...]
    ys = []
    for c in range(ts // L):
        sl = slice(c * L, (c + 1) * L)
        cs = _exact_lhs_dot(tri, a[sl], wide=False)
        cs_t = cs.T
        cs_last = cs[L - 1:L, :]
        y_groups = []
        for grp in range(2):
            h0 = 2 * grp
            b_g = bm[sl, grp * SSM_STATE:(grp + 1) * SSM_STATE].astype(BF16)
            c_g = cm[sl, grp * SSM_STATE:(grp + 1) * SSM_STATE].astype(BF16)
            x_g = xdt[sl, grp * LANES:(grp + 1) * LANES]
            cb = _dot_nt(c_g, b_g)
            ws = []
            for h in (h0, h0 + 1):
                colh = cs[:, SMALL_DT + h:SMALL_DT + h + 1]
                rowh = cs_t[SMALL_DT + h:SMALL_DT + h + 1, :]
                ws.append(cb * jnp.exp(jnp.where(jj <= ii, colh - rowh, NEG)))
            yd = _dot(jnp.concatenate(ws, axis=0).astype(BF16), x_g.astype(BF16))
            y_diag = jnp.where(lane_lo, yd[0:L], yd[L:2 * L])
            cs_g = _head_cols(cs, SMALL_DT + h0, L, HEAD_DIM, 2)
            st_g = st_ref[grp * LANES:(grp + 1) * LANES, :]
            y_off = _dot_nt(c_g, st_g.astype(BF16)) * jnp.exp(cs_g)
            y_groups.append(y_diag + y_off)
            last_g = _head_cols(cs_last, SMALL_DT + h0, 1, HEAD_DIM, 2)
            x_dec = x_g * jnp.exp(last_g - cs_g)
            new = _dot_tn(x_dec.astype(BF16), b_g)
            e0 = jnp.exp(cs_last[:, SMALL_DT + h0:SMALL_DT + h0 + 1])
            e1 = jnp.exp(cs_last[:, SMALL_DT + h0 + 1:SMALL_DT + h0 + 2])
            st_ref[grp * LANES:(grp + 1) * LANES, :] = jnp.where(row_lo, e0, e1) * st_g + new
        ys.append(jnp.concatenate(y_groups, axis=1))
    y = jnp.concatenate(ys, axis=0)
    y = (y + dskip_ref[...] * xs) * _silu(z)
    halves = []
    for grp in range(2):
        yg = y[:, grp * LANES:(grp + 1) * LANES]
        halves.append(_rms(yg, nw_ref[:, grp * LANES:(grp + 1) * LANES]))
    o_ref[...] = jnp.concatenate(halves, axis=1).astype(o_ref.dtype)


def _ssd(ssm, small, cw, cb, dtb, alog, dskip, nw, b, s):
    ts = SCAN_ROWS
    ns = s // ts
    tri = jnp.asarray(np.tril(np.ones((SSM_CHUNK, SSM_CHUNK), np.float32)), BF16)
    row = lambda bi, i: (bi * ns + i, 0)
    full = lambda bi, i: (0, 0)
    return pl.pallas_call(
        _ssd_body,
        grid=(b, ns),
        in_specs=[pl.BlockSpec((ts, 1024), row),
                  pl.BlockSpec((ts, LANES), row),
                  pl.BlockSpec((SSM_CONV, SSM_CONV_DIM), full),
                  pl.BlockSpec((1, SSM_CONV_DIM), full),
                  pl.BlockSpec((1, LANES), full),
                  pl.BlockSpec((1, LANES), full),
                  pl.BlockSpec((1, GROUP_WIDTH), full),
                  pl.BlockSpec((1, GROUP_WIDTH), full),
                  pl.BlockSpec((SSM_CHUNK, SSM_CHUNK), full)],
        out_specs=pl.BlockSpec((ts, GROUP_WIDTH), row),
        out_shape=jax.ShapeDtypeStruct((b * s, GROUP_WIDTH), BF16),
        scratch_shapes=[pltpu.VMEM((8, SSM_CONV_DIM), F32), pltpu.VMEM((GROUP_WIDTH, SSM_STATE), F32)],
        compiler_params=_cparams(("parallel", "arbitrary")),
        name="ssd",
    )(ssm, small, cw, cb, dtb, alog, dskip, nw, tri)


def _out_ffn_body(x_ref, yaT_ref, yb_ref, ycT_ref, yd_ref, wo_ref, n2_ref, wg_ref, wu_ref, wd_ref,
                  fn_ref, o_ref, *, final_norm):
    tm = x_ref.shape[0]
    ya_t = yaT_ref[0, :, 0].reshape(GROUP_WIDTH, tm)
    yc_t = ycT_ref[0, :, 0].reshape(GROUP_WIDTH, tm)
    x1 = (x_ref[...]
          + _dot_tn(ya_t, wo_ref[0]) + _dot(yb_ref[...], wo_ref[1])
          + _dot_tn(yc_t, wo_ref[2]) + _dot(yd_ref[...], wo_ref[3]))
    h2 = _rms(x1, n2_ref[...]).astype(BF16)
    hidden = _silu(_dot(h2, wg_ref[...])) * _dot(h2, wu_ref[...])
    out = x1 + _dot(hidden.astype(BF16), wd_ref[...])
    if final_norm:
        out = _rms(out, fn_ref[...])
    o_ref[...] = out


def _out_ffn(x2, yaT, yb, ycT, yd, wo, n2, wg, wu, wd, fn, b, s, final_norm):
    tm = TILE_ROWS
    ns = s // tm
    row = lambda i: (i, 0)
    tr = lambda i: (i // ns, 0, i % ns, 0, 0)
    att_blk = (1, HEADS, 1, HEAD_DIM, tm)

    def resident(shape):
        return pl.BlockSpec(shape, lambda i: (0,) * len(shape), pipeline_mode=pl.Buffered(1))

    return pl.pallas_call(
        functools.partial(_out_ffn_body, final_norm=final_norm),
        grid=(b * ns,),
        in_specs=[pl.BlockSpec((tm, D_MODEL), row),
                  pl.BlockSpec(att_blk, tr),
                  pl.BlockSpec((tm, GROUP_WIDTH), row),
                  pl.BlockSpec(att_blk, tr),
                  pl.BlockSpec((tm, GROUP_WIDTH), row),
                  resident((4, GROUP_WIDTH, D_MODEL)),
                  resident((1, D_MODEL)),
                  resident((D_MODEL, FFN_HIDDEN)),
                  resident((D_MODEL, FFN_HIDDEN)),
                  resident((FFN_HIDDEN, D_MODEL)),
                  resident((1, D_MODEL))],
        out_specs=pl.BlockSpec((tm, D_MODEL), row),
        out_shape=jax.ShapeDtypeStruct((b * s, D_MODEL), F32),
        compiler_params=_cparams(("parallel",)),
        name="out_ffn",
    )(x2, yaT, yb, ycT, yd, wo, n2, wg, wu, wd, fn)


def _prep_w_in(w):
    z = lambda n: jnp.zeros((w.shape[0], n), w.dtype)
    small = jnp.concatenate([w[:, 768:772], w[:, 2996:3000], z(8), w[:, 1540:1556], z(32),
                             w[:, 1940:1972], z(32)], axis=1)
    out = jnp.concatenate([w[:, 0:768], w[:, 772:1540], w[:, 1556:1940], w[:, 1972:2996], small], axis=1)
    return out.astype(BF16)


def _lane_row(vals, lane0):
    return jnp.zeros((1, LANES), F32).at[0, lane0:lane0 + vals.shape[0]].set(vals.astype(F32))


def _prep_mla_w(w_uq, w_ukv):
    wq = w_uq.reshape(MLA_Q_LORA, HEADS, HEAD_DIM + MLA_ROPE)
    wq = jnp.pad(wq, ((0, 0), (0, 0), (0, LANES - HEAD_DIM - MLA_ROPE))).reshape(MLA_Q_LORA, HEADS * LANES)
    wkv = w_ukv.reshape(MLA_KV_LORA, HEADS, 2 * HEAD_DIM)
    wk = jnp.pad(wkv[:, :, :HEAD_DIM], ((0, 0), (0, 0), (0, LANES - HEAD_DIM))).reshape(MLA_KV_LORA, HEADS * LANES)
    wv = wkv[:, :, HEAD_DIM:].reshape(MLA_KV_LORA, GROUP_WIDTH)
    return wq.astype(BF16), wk.astype(BF16), wv.astype(BF16)


def kernel(x, positions, norm1, w_in, fox_f_bias, gla_gate_w2, gla_gate_b, gla_out_norm, mla_q_norm,
           mla_w_uq, mla_kv_norm, mla_w_ukv, ssm_conv_w, ssm_conv_b, ssm_dt_bias, ssm_A_log, ssm_D,
           ssm_norm, w_out, norm2, w_gate, w_up, w_down, final_norm):
    b, s, d = x.shape
    depth = w_in.shape[0]
    assert d == D_MODEL and s % TILE_ROWS == 0 and ATT_TQ == TILE_ROWS and ATT_TQ % ATT_TK == 0
    t = b * s
    x2 = x.reshape(t, d)

    half = MLA_ROPE // 2
    inv = ROPE_THETA ** (-jnp.arange(half, dtype=F32) / half)
    invf = jnp.zeros((1, LANES), F32).at[0, 64:64 + MLA_ROPE].set(jnp.concatenate([inv, inv]))
    cos_t, sin_t = _rope_tables(positions.reshape(t, 1), invf)

    tri_att = jnp.asarray(np.tril(np.ones((CUMSUM_ROWS, CUMSUM_ROWS), np.float32)), BF16)

    for l in range(depth):
        foxq, foxkv, gla, mla, ssm, small = _inproj(x2, norm1[l][None, :], _prep_w_in(w_in[l]))

        qT, kA, vT = _fox_prep(foxq, foxkv, small, _lane_row(fox_f_bias[l], SMALL_FOX_F), tri_att, b, s)
        ya = _flash(qT, kA, vT)

        wq, wk, wv = _prep_mla_w(mla_w_uq[l], mla_w_ukv[l])
        qT, kA, vT = _mla_prep(mla, small, cos_t, sin_t, mla_q_norm[l][None, :], mla_kv_norm[l][None, :],
                               wq, wk, wv, b, s)
        yc = _flash(qT, kA, vT)

        w2p = jnp.zeros((LANES, HEADS * GLA_DK), F32).at[SMALL_GATE:SMALL_GATE + GLA_GATE_RANK].set(
            gla_gate_w2[l]).astype(BF16)
        yb = _gla(gla, small, w2p, gla_gate_b[l][None, :].astype(F32),
                  jnp.tile(gla_out_norm[l], HEADS)[None, :].astype(F32), b, s)

        yd = _ssd(ssm, small, ssm_conv_w[l], ssm_conv_b[l][None, :],
                  _lane_row(ssm_dt_bias[l], SMALL_DT), _lane_row(ssm_A_log[l], SMALL_DT),
                  jnp.repeat(ssm_D[l], HEAD_DIM)[None, :].astype(F32), ssm_norm[l][None, :], b, s)

        x2 = _out_ffn(x2, ya, yb, yc, yd, w_out[l].reshape(4, GROUP_WIDTH, D_MODEL).astype(BF16),
                      norm2[l][None, :], w_gate[l].astype(BF16), w_up[l].astype(BF16),
                      w_down[l].astype(BF16), final_norm[None, :], b, s,
                      final_norm=(l == depth - 1))
    return x2.reshape(b, s, d)
```

```python
import functools
import math

import numpy as np
import jax
import jax.numpy as jnp
from jax import lax
from jax.experimental import pallas as pl
from jax.experimental.pallas import tpu as pltpu

F32 = jnp.float32
BF16 = jnp.bfloat16

D_MODEL = 1024
GROUP_WIDTH = 256
HEADS = 4
HEAD_DIM = 64
GLA_DK = 32
GLA_GATE_RANK = 16
GLA_GATE_TAU = 16.0
GLA_CHUNK = 64
GLA_SUB = 16
MLA_ROPE = 32
MLA_Q_LORA = 256
MLA_KV_LORA = 128
ROPE_THETA = 10000.0
SSM_STATE = 128
SSM_CONV = 4
SSM_CHUNK = 128
SSM_CONV_DIM = 768
FFN_HIDDEN = 2816
EPS = 1e-6
NEG = -1e30
LOG2E = math.log2(math.e)
SUM_ROWS = 16

LANES = 128
QK_PAD = 128
VMEM_LIMIT = 56 * 1024 * 1024

SEG_FOXQ = (0, 256)
SEG_FOXKV = (256, 768)
SEG_GLA = (768, 1536)
SEG_MLA = (1536, 1920)
SEG_SSM = (1920, 2944)
SEG_SMALL = (2944, 3072)
IN_PAD = 3072
SMALL_FOX_F = 0
SMALL_DT = 4
SMALL_GATE = 16
SMALL_KROPE = 64

TILE_ROWS = 512
ATT_TQ = 512
ATT_TK = 256
FLASH_UNROLL = 4
SCAN_ROWS = 512
GLA_ROWS = 512
CUMSUM_ROWS = 256


def _cparams(sem):
    return pltpu.CompilerParams(dimension_semantics=sem, vmem_limit_bytes=VMEM_LIMIT)


def _rms(x, g):
    ms = jnp.mean(x * x, axis=-1, keepdims=True)
    return x * lax.rsqrt(ms + EPS) * g


def _log_sigmoid(x):
    return jnp.minimum(x, 0.0) - jnp.log1p(jnp.exp(-jnp.abs(x)))


def _softplus(x):
    return jnp.maximum(x, 0.0) + jnp.log1p(jnp.exp(-jnp.abs(x)))


def _silu(x):
    return x / (1.0 + jnp.exp(-x))


def _split3(x):
    hi = x.astype(BF16)
    r = x - hi.astype(F32)
    mid = r.astype(BF16)
    lo = (r - mid.astype(F32)).astype(BF16)
    return hi, mid, lo


def _dot(a, b):
    return jnp.dot(a, b, preferred_element_type=F32)


def _dot_nt(a, b):
    return lax.dot_general(a, b, (((1,), (1,)), ((), ())), preferred_element_type=F32)


def _dot_tn(a, b):
    return lax.dot_general(a, b, (((0,), (0,)), ((), ())), preferred_element_type=F32)


def _exact_lhs_dot(m01, x, wide=True):
    hi, mid, lo = _split3(x)
    if not wide:
        return _dot(m01, hi) + _dot(m01, mid) + _dot(m01, lo)
    w = x.shape[1]
    y = _dot(m01, jnp.concatenate([hi, mid, lo], axis=1))
    return y[:, 0:w] + y[:, w:2 * w] + y[:, 2 * w:3 * w]


def _inproj_body(x_ref, g_ref, w_ref, foxq_ref, foxkv_ref, gla_ref, mla_ref, ssm_ref, small_ref):
    h = _rms(x_ref[...], g_ref[...]).astype(BF16)

    def seg(s):
        return _dot(h, w_ref[:, s[0]:s[1]])

    foxq_ref[...] = seg(SEG_FOXQ)
    foxkv_ref[...] = seg(SEG_FOXKV).astype(BF16)
    gla_ref[...] = seg(SEG_GLA)
    mla_ref[...] = seg(SEG_MLA)
    ssm_ref[...] = seg(SEG_SSM)
    small_ref[...] = seg(SEG_SMALL)


def _inproj(x2, g, w):
    t = x2.shape[0]
    tm = TILE_ROWS
    widths = [s[1] - s[0] for s in (SEG_FOXQ, SEG_FOXKV, SEG_GLA, SEG_MLA, SEG_SSM, SEG_SMALL)]
    dts = [F32, BF16, F32, F32, F32, F32]
    return pl.pallas_call(
        _inproj_body,
        grid=(t // tm,),
        in_specs=[pl.BlockSpec((tm, D_MODEL), lambda i: (i, 0)),
                  pl.BlockSpec((1, D_MODEL), lambda i: (0, 0)),
                  pl.BlockSpec((D_MODEL, IN_PAD), lambda i: (0, 0))],
        out_specs=[pl.BlockSpec((tm, w_), lambda i: (i, 0)) for w_ in widths],
        out_shape=[jax.ShapeDtypeStruct((t, w_), d_) for w_, d_ in zip(widths, dts)],
        compiler_params=_cparams(("parallel",)),
        name="inproj",
    )(x2, g, w)


def _rope_table_body(pos_ref, invf_ref, cos_ref, sin_ref):
    tm = pos_ref.shape[0]
    groups = LANES // MLA_ROPE
    q = tm // groups
    lane = lax.broadcasted_iota(jnp.int32, (q, LANES), 1)
    pos = pos_ref[...].astype(F32)
    packed = jnp.zeros((q, LANES), F32)
    for m in range(groups):
        packed = jnp.where(lane // MLA_ROPE == m, jnp.broadcast_to(pos[m * q:(m + 1) * q], (q, LANES)), packed)
    ang = packed * invf_ref[...]
    cos_p, sin_p = jnp.cos(ang), jnp.sin(ang)
    inside = (lane >= 64) & (lane < 64 + MLA_ROPE)
    for m in range(groups):
        shift = (64 - MLA_ROPE * m) % LANES
        move = (lambda t: t) if shift == 0 else (lambda t: pltpu.roll(t, shift, 1))
        cos_ref[m * q:(m + 1) * q, :] = jnp.where(inside, move(cos_p), 1.0)
        sin_ref[m * q:(m + 1) * q, :] = jnp.where(inside, move(sin_p), 0.0)


def _rope_tables(pos_col, invf):
    t = pos_col.shape[0]
    tm = TILE_ROWS
    return pl.pallas_call(
        _rope_table_body,
        grid=(t // tm,),
        in_specs=[pl.BlockSpec((tm, 1), lambda i: (i, 0)),
                  pl.BlockSpec((1, LANES), lambda i: (0, 0))],
        out_specs=[pl.BlockSpec((tm, LANES), lambda i: (i, 0))] * 2,
        out_shape=[jax.ShapeDtypeStruct((t, LANES), F32)] * 2,
        compiler_params=_cparams(("parallel",)),
        name="rope_tables",
    )(pos_col, invf)


def _transpose_to_bf16(x):
    eye = (lax.broadcasted_iota(jnp.int32, (LANES, LANES), 0)
           == lax.broadcasted_iota(jnp.int32, (LANES, LANES), 1)).astype(BF16)
    return _dot_nt(eye, x.astype(BF16)).astype(BF16)


def _store_vT(vT_ref, v_all):
    for blk in range(v_all.shape[0] // ATT_TK):
        for pair in range(2):
            vp = v_all[blk * ATT_TK:(blk + 1) * ATT_TK, pair * LANES:(pair + 1) * LANES]
            vT_ref[0, 0, blk, pair * LANES:(pair + 1) * LANES, :] = _transpose_to_bf16(vp)


def _fox_prep_body(foxq_ref, foxkv_ref, small_ref, fb_ref, tri_ref, qT_ref, kA_ref, vT_ref, carry_ref):
    ts = foxq_ref.shape[0]

    @pl.when(pl.program_id(1) == 0)
    def _():
        carry_ref[...] = jnp.zeros_like(carry_ref)

    logf = _log_sigmoid(small_ref[...] + fb_ref[...])
    blk = tri_ref.shape[0]
    pieces, run = [], carry_ref[...]
    for i0 in range(0, ts, blk):
        part = _exact_lhs_dot(tri_ref[...], logf[i0:i0 + blk]) + run
        run = part[blk - 1:blk, :]
        pieces.append(part)
    fcum = jnp.concatenate(pieces, axis=0)
    carry_ref[...] = run
    f_hi, f_mid, f_lo = [p.astype(F32) for p in _split3(fcum * LOG2E)]

    lane = lax.broadcasted_iota(jnp.int32, (ts, LANES), 1)
    low = lane < HEAD_DIM
    packed = jnp.where(lane < HEADS, f_hi,
             jnp.where(lane < 2 * HEADS, pltpu.roll(f_mid, HEADS, 1), pltpu.roll(f_lo, 2 * HEADS, 1)))
    f_q = pltpu.roll(packed, HEAD_DIM, 1)
    f_k = -pltpu.roll(packed, HEAD_DIM + 3 * HEADS, 1)
    _store_vT(vT_ref, foxkv_ref[:, 256:512].astype(F32))
    for pair in range(2):
        qp = foxq_ref[:, pair * LANES:(pair + 1) * LANES] * (LOG2E * HEAD_DIM ** -0.5)
        kp = foxkv_ref[:, pair * LANES:(pair + 1) * LANES].astype(F32)
        for e in range(2):
            h = 2 * pair + e
            qh = qp if e == 0 else pltpu.roll(qp, HEAD_DIM, 1)
            kh = kp if e == 0 else pltpu.roll(kp, HEAD_DIM, 1)
            mine = (lane - HEAD_DIM) % HEADS == h
            first = mine & (lane >= HEAD_DIM) & (lane < HEAD_DIM + 3 * HEADS)
            second = mine & (lane >= HEAD_DIM + 3 * HEADS) & (lane < HEAD_DIM + 6 * HEADS)
            qa = jnp.where(low, qh, jnp.where(first, f_q, jnp.where(second, 1.0, 0.0)))
            ka = jnp.where(low, kh, jnp.where(first, 1.0, jnp.where(second, f_k, 0.0)))
            qT_ref[0, h, 0] = _transpose_to_bf16(qa)
            kA_ref[0, h] = ka.astype(BF16)


def _att_out_shapes(b, s, ts):
    return [jax.ShapeDtypeStruct((b, HEADS, s // ts, QK_PAD, ts), BF16),
            jax.ShapeDtypeStruct((b, HEADS, s, QK_PAD), BF16),
            jax.ShapeDtypeStruct((b, s // ts, ts // ATT_TK, GROUP_WIDTH, ATT_TK), BF16)]


def _att_out_specs(ts):
    return [pl.BlockSpec((1, HEADS, 1, QK_PAD, ts), lambda b, i: (b, 0, i, 0, 0)),
            pl.BlockSpec((1, HEADS, ts, QK_PAD), lambda b, i: (b, 0, i, 0)),
            pl.BlockSpec((1, 1, ts // ATT_TK, GROUP_WIDTH, ATT_TK), lambda b, i: (b, i, 0, 0, 0))]


def _fox_prep(foxq, foxkv, small, fbias, tri, b, s):
    ts = TILE_ROWS
    ns = s // ts
    return pl.pallas_call(
        _fox_prep_body,
        grid=(b, ns),
        in_specs=[pl.BlockSpec((ts, 256), lambda bi, i: (bi * ns + i, 0)),
                  pl.BlockSpec((ts, 512), lambda bi, i: (bi * ns + i, 0)),
                  pl.BlockSpec((ts, LANES), lambda bi, i: (bi * ns + i, 0)),
                  pl.BlockSpec((1, LANES), lambda bi, i: (0, 0)),
                  pl.BlockSpec(tri.shape, lambda bi, i: (0, 0))],
        out_specs=_att_out_specs(ts),
        out_shape=_att_out_shapes(b, s, ts),
        scratch_shapes=[pltpu.VMEM((1, LANES), F32)],
        compiler_params=_cparams(("parallel", "arbitrary")),
        name="fox_prep",
    )(foxq, foxkv, small, fbias, tri)


def _mla_prep_body(mla_ref, small_ref, cos_ref, sin_ref, qn_ref, kvn_ref, wq_ref, wk_ref, wv_ref,
                   qT_ref, kA_ref, vT_ref):
    ts = mla_ref.shape[0]
    lane = lax.broadcasted_iota(jnp.int32, (ts, LANES), 1)
    cosv = cos_ref[...]
    sinv = sin_ref[...]
    half = MLA_ROPE // 2
    sin_a = jnp.where((lane >= 64) & (lane < 64 + half), -sinv, 0.0)
    sin_b = jnp.where((lane >= 64 + half) & (lane < 64 + MLA_ROPE), sinv, 0.0)

    def rope(x):
        return (x * cosv + pltpu.roll(x, LANES - half, 1) * sin_a + pltpu.roll(x, half, 1) * sin_b)

    cq = _rms(mla_ref[:, 0:MLA_Q_LORA], qn_ref[...]).astype(BF16)
    ckv = _rms(mla_ref[:, MLA_Q_LORA:MLA_Q_LORA + MLA_KV_LORA], kvn_ref[...]).astype(BF16)
    kr = rope(jnp.where((lane >= 64) & (lane < 64 + MLA_ROPE), small_ref[...], 0.0))
    q_all = _dot(cq, wq_ref[...]) * (LOG2E * (HEAD_DIM + MLA_ROPE) ** -0.5)
    k_all = _dot(ckv, wk_ref[...])
    for h in range(HEADS):
        qh = rope(q_all[:, h * LANES:(h + 1) * LANES])
        qT_ref[0, h, 0] = _transpose_to_bf16(qh)
        kA_ref[0, h] = (k_all[:, h * LANES:(h + 1) * LANES] + kr).astype(BF16)
    _store_vT(vT_ref, _dot(ckv, wv_ref[...]))


def _mla_prep(mla, small, cos_t, sin_t, qn, kvn, wq, wk, wv, b, s):
    ts = TILE_ROWS
    ns = s // ts
    row = lambda bi, i: (bi * ns + i, 0)
    full = lambda bi, i: (0, 0)
    return pl.pallas_call(
        _mla_prep_body,
        grid=(b, ns),
        in_specs=[pl.BlockSpec((ts, 384), row),
                  pl.BlockSpec((ts, LANES), row),
                  pl.BlockSpec((ts, LANES), row),
                  pl.BlockSpec((ts, LANES), row),
                  pl.BlockSpec((1, MLA_Q_LORA), full),
                  pl.BlockSpec((1, MLA_KV_LORA), full),
                  pl.BlockSpec((MLA_Q_LORA, HEADS * LANES), full),
                  pl.BlockSpec((MLA_KV_LORA, HEADS * LANES), full),
                  pl.BlockSpec((MLA_KV_LORA, GROUP_WIDTH), full)],
        out_specs=_att_out_specs(ts),
        out_shape=_att_out_shapes(b, s, ts),
        compiler_params=_cparams(("parallel", "parallel")),
        name="mla_prep",
    )(mla, small, cos_t, sin_t, qn, kvn, wq, wk, wv)


def _flash_body(qT_ref, k_ref, vT_ref, o_ref, m_ref, acc_ref, s_ref, p_ref, a_ref, mb_ref, *, tq, tk):
    nq = qT_ref.shape[2]
    nb = tq // tk
    assert nb % 2 == 0
    ones = jnp.ones((SUM_ROWS, tk), BF16)

    def score_block(qi, j, slot, lo=0, with_max=True):
        start = pl.multiple_of(j * tk, tk)
        s = _dot(k_ref[0, 0, pl.ds(start, tk), :], qT_ref[0, 0, qi, :, lo:])
        s_ref[slot, :, lo:] = s
        if with_max:
            mb_ref[slot] = jnp.max(s, axis=0, keepdims=True)

    score_block(0, 0, 0)

    def q_tile(i, carry):
        m_ref[...] = jnp.full_like(m_ref, NEG)
        acc_ref[...] = jnp.zeros_like(acc_ref)
        p_ref[1] = jnp.zeros_like(p_ref[1])
        a_ref[1] = jnp.ones_like(a_ref[1])
        n_full = nb * i

        def scores(j, slot, lo=0, with_max=True):
            score_block(i, j, slot, lo, with_max)

        def softmax(slot, lo=0, diag=False):
            s = s_ref[slot, :, lo:]
            if diag:
                keep = (lax.broadcasted_iota(jnp.int32, (tk, tk), 0)
                        <= lax.broadcasted_iota(jnp.int32, (tk, tk), 1))
                left = jnp.where(keep, s[:, :tk], NEG)
                s = left if s.shape[1] == tk else jnp.concatenate([left, s[:, tk:]], axis=1)
                m_blk = jnp.max(s, axis=0, keepdims=True)
            else:
                m_blk = mb_ref[slot]
            m_prev = m_ref[:, lo:]
            m_new = jnp.maximum(m_prev, m_blk)
            a_ref[slot, :, lo:] = jnp.exp2(m_prev - m_new)
            p_ref[slot, :, lo:] = jnp.exp2(s - m_new).astype(BF16)
            m_ref[:, lo:] = m_new

        def pv(j, slot, lo=0):
            v_ext = jnp.concatenate([vT_ref[0, j, 0], ones], axis=0)
            acc_ref[:, lo:] = (a_ref[slot, :, lo:] * acc_ref[:, lo:]
                               + _dot(v_ext, p_ref[slot, :, lo:]))

        def run_blocks(j, count):
            for u in range(count):
                scores(j + u + 1, (u + 1) % 2)
                softmax(u % 2)
                pv(jnp.maximum(j + u - 1, 0), (u + 1) % 2)

        per_trip = nb * FLASH_UNROLL
        n_trips = i // FLASH_UNROLL

        def body(t, c2):
            run_blocks(per_trip * t, per_trip)
            return c2

        lax.fori_loop(0, n_trips, body, 0)
        done = per_trip * n_trips
        part = FLASH_UNROLL // 2
        while part >= 1:
            take = (i % (2 * part)) >= part

            @pl.when(take)
            def _(done=done, part=part):
                run_blocks(done, nb * part)

            done = done + jnp.where(take, nb * part, 0)
            part //= 2
        for u in range(nb):
            if u + 1 < nb:
                scores(n_full + u + 1, (u + 1) % 2, lo=(u + 1) * tk, with_max=False)
            softmax(u % 2, lo=u * tk, diag=True)
            pv(jnp.maximum(n_full + u - 1, 0), (u + 1) % 2, lo=max(u - 1, 0) * tk)
        score_block(jnp.minimum(i + 1, nq - 1), 0, 0)
        pv(n_full + nb - 1, (nb - 1) % 2, lo=(nb - 1) * tk)
        o_ref[0, 0, i] = (acc_ref[0:HEAD_DIM, :] / acc_ref[HEAD_DIM:HEAD_DIM + 1, :]).astype(o_ref.dtype)
        return carry

    lax.fori_loop(0, nq, q_tile, 0)


def _flash(qT, kA, vT):
    b, _, nq, _, tq = qT.shape
    tk = ATT_TK
    s = nq * tq
    nk = s // tk
    vT5 = vT.reshape(b, nk, HEADS, HEAD_DIM, tk)
    return pl.pallas_call(
        functools.partial(_flash_body, tq=tq, tk=tk),
        grid=(b, HEADS),
        in_specs=[pl.BlockSpec((1, 1, nq, QK_PAD, tq), lambda bi, h: (bi, h, 0, 0, 0)),
                  pl.BlockSpec((1, 1, s, QK_PAD), lambda bi, h: (bi, h, 0, 0)),
                  pl.BlockSpec((1, nk, 1, HEAD_DIM, tk), lambda bi, h: (bi, 0, h, 0, 0))],
        out_specs=pl.BlockSpec((1, 1, nq, HEAD_DIM, tq), lambda bi, h: (bi, h, 0, 0, 0)),
        out_shape=jax.ShapeDtypeStruct((b, HEADS, nq, HEAD_DIM, tq), BF16),
        scratch_shapes=[pltpu.VMEM((1, tq), F32),
                        pltpu.VMEM((HEAD_DIM + SUM_ROWS, tq), F32),
                        pltpu.VMEM((2, tk, tq), F32),
                        pltpu.VMEM((2, tk, tq), BF16),
                        pltpu.VMEM((2, 1, tq), F32),
                        pltpu.VMEM((2, 1, tq), F32)],
        compiler_params=_cparams(("parallel", "parallel")),
        name="flash",
    )(qT, kA, vT5)


def _gla_consts(ts):
    idx = np.arange(ts)
    same = (idx[:, None] // GLA_CHUNK) == (idx[None, :] // GLA_CHUNK)
    mbig = (same & (idx[None, :] <= idx[:, None])).astype(np.float32)
    hd = np.arange(HEADS * GLA_DK) // GLA_DK
    hv = np.arange(GROUP_WIDTH) // HEAD_DIM
    ind_dk_dv = (hd[:, None] == hv[None, :]).astype(np.float32)
    ind_mean = (hv[:, None] == hv[None, :]).astype(np.float32) / HEAD_DIM
    return (jnp.asarray(mbig, BF16), jnp.asarray(ind_dk_dv, BF16), jnp.asarray(ind_mean, BF16))


def _pick_heads(stacked, rows):
    lane_h = lax.broadcasted_iota(jnp.int32, (rows, GROUP_WIDTH), 1) // HEAD_DIM
    out = jnp.zeros((rows, GROUP_WIDTH), F32)
    for h in range(HEADS):
        out = jnp.where(lane_h == h, stacked[h * rows:(h + 1) * rows, :], out)
    return out


def _gla_body(gla_ref, small_ref, w2_ref, b2_ref, on_ref, mbig_ref, ind_ref, indm_ref, o_ref, st_ref,
              gpad_ref, kpad_ref, vpad_ref):
    ts = gla_ref.shape[0]
    L = GLA_CHUNK
    nsub = L // GLA_SUB
    PAD = GLA_SUB

    @pl.when(pl.program_id(1) == 0)
    def _():
        st_ref[...] = jnp.zeros_like(st_ref)

    q = gla_ref[:, 0:128] * (GLA_DK ** -0.5)
    k = gla_ref[:, 128:256]
    v = gla_ref[:, 256:512]
    r = gla_ref[:, 512:768]
    x = _dot(small_ref[...].astype(BF16), w2_ref[...]) + b2_ref[...]
    g = _log_sigmoid(x) * (1.0 / GLA_GATE_TAU)
    cb = mbig_ref.shape[0]
    G = jnp.concatenate([_exact_lhs_dot(mbig_ref[...], g[i0:i0 + cb]) for i0 in range(0, ts, cb)], axis=0)
    r_rows, gl_rows = [], []
    for c in range(ts // L):
        for I in range(nsub):
            at = c * L + I * GLA_SUB
            r_rows.append(jnp.zeros((GLA_SUB, 128), F32) if I == 0
                          else jnp.broadcast_to(G[at - 1:at, :], (GLA_SUB, 128)))
        gl_rows.append(jnp.broadcast_to(G[(c + 1) * L - 1:(c + 1) * L, :], (L, 128)))
    R = jnp.concatenate(r_rows, axis=0)
    GL = jnp.concatenate(gl_rows, axis=0)
    q_inter = q * jnp.exp(G)
    q_norm = q * jnp.exp(G - R)
    k_dec = k * jnp.exp(GL - G)
    for ref, val in ((gpad_ref, G), (kpad_ref, k), (vpad_ref, v)):
        ref[0:PAD, :] = jnp.zeros((PAD, val.shape[1]), F32)
        ref[PAD:PAD + ts, :] = val

    lane_h = lax.broadcasted_iota(jnp.int32, (L, 128), 1) // GLA_DK
    row = lax.broadcasted_iota(jnp.int32, (L, 128), 0)
    sub_pos = row % GLA_SUB
    sub_idx = row // GLA_SUB
    row_blk = lax.broadcasted_iota(jnp.int32, (L, L * 2), 0) // GLA_SUB
    zeros_v = jnp.zeros((L, GROUP_WIDTH), BF16)

    def stack_heads(a):
        return jnp.concatenate([jnp.where(lane_h == h, a, 0.0) for h in range(HEADS)], axis=0)

    outs = []
    for c in range(ts // L):
        sl = slice(c * L, (c + 1) * L)
        Gc, kc, vc, qc = G[sl], k[sl], v[sl], q[sl]
        vc16 = vc.astype(BF16)

        r_next = jnp.concatenate([R[c * L + GLA_SUB:(c + 1) * L], R[(c + 1) * L - GLA_SUB:(c + 1) * L]],
                                 axis=0)
        base = kc * jnp.exp(jnp.where(sub_idx < nsub - 1, r_next - Gc, NEG))
        kts = []
        kt = jnp.zeros_like(base)
        for I in range(1, nsub):
            if I > 1:
                r_i = R[c * L + I * GLA_SUB:c * L + I * GLA_SUB + 1, :]
                r_p = R[c * L + (I - 1) * GLA_SUB:c * L + (I - 1) * GLA_SUB + 1, :]
                kt = kt * jnp.exp(r_i - r_p)
            kt = jnp.where(sub_idx == I - 1, base, kt)
            kts += [kt, jnp.zeros_like(kt)]
        kstack = jnp.concatenate(kts, axis=0).astype(BF16)
        sc = _dot_nt(stack_heads(q_norm[sl]).astype(BF16), kstack)
        a_rows = []
        for h in range(HEADS):
            sh = sc[h * L:(h + 1) * L]
            a_h = jnp.zeros((L, 2 * L), F32)
            for I in range(1, nsub):
                a_h = jnp.where(row_blk == I, sh[:, (I - 1) * 2 * L:I * 2 * L], a_h)
            a_rows.append(a_h)
        a_st = jnp.concatenate(a_rows, axis=0).astype(BF16)
        v_pad = jnp.concatenate([vc16, zeros_v], axis=0)
        st = st_ref[...]
        big = _dot(a_st, v_pad) + _dot_nt(stack_heads(q_inter[sl]).astype(BF16), st.astype(BF16))
        o_c = _pick_heads(big, L)

        ps = []
        for d in range(GLA_SUB):
            if d == 0:
                p = qc * kc
            else:
                lo = PAD + c * L - d
                arg = jnp.where(sub_pos >= d, Gc - gpad_ref[lo:lo + L, :], NEG)
                p = qc * kpad_ref[lo:lo + L, :] * jnp.exp(arg)
            ps.append(p)
        abc = _dot(jnp.concatenate(ps, axis=0).astype(BF16), ind_ref[...])
        for d in range(GLA_SUB):
            lo = PAD + c * L - d
            o_c = o_c + abc[d * L:(d + 1) * L] * vpad_ref[lo:lo + L, :]
        outs.append(o_c)

        st_ref[...] = (jnp.exp(GL[c * L:c * L + 1, :]) * st
                       + _dot_tn(vc16, k_dec[sl].astype(BF16)))

    o = jnp.concatenate(outs, axis=0)
    o2 = o * o
    hi = o2.astype(BF16)
    lo = (o2 - hi.astype(F32)).astype(BF16)
    ms = _dot(hi, indm_ref[...]) + _dot(lo, indm_ref[...])
    o_ref[...] = (o * lax.rsqrt(ms + EPS) * on_ref[...] * _silu(r)).astype(o_ref.dtype)


def _gla(gla, small, w2p, b2, onorm, b, s):
    ts = GLA_ROWS
    ns = s // ts
    mbig, ind, indm = _gla_consts(CUMSUM_ROWS)
    row = lambda bi, i: (bi * ns + i, 0)
    full = lambda bi, i: (0, 0)
    return pl.pallas_call(
        _gla_body,
        grid=(b, ns),
        in_specs=[pl.BlockSpec((ts, 768), row),
                  pl.BlockSpec((ts, LANES), row),
                  pl.BlockSpec((LANES, 128), full),
                  pl.BlockSpec((1, 128), full),
                  pl.BlockSpec((1, GROUP_WIDTH), full),
                  pl.BlockSpec(mbig.shape, full),
                  pl.BlockSpec((128, GROUP_WIDTH), full),
                  pl.BlockSpec((GROUP_WIDTH, GROUP_WIDTH), full)],
        out_specs=pl.BlockSpec((ts, GROUP_WIDTH), row),
        out_shape=jax.ShapeDtypeStruct((b * s, GROUP_WIDTH), BF16),
        scratch_shapes=[pltpu.VMEM((GROUP_WIDTH, 128), F32),
                        pltpu.VMEM((GLA_SUB + ts, 128), F32),
                        pltpu.VMEM((GLA_SUB + ts, 128), F32),
                        pltpu.VMEM((GLA_SUB + ts, GROUP_WIDTH), F32)],
        compiler_params=_cparams(("parallel", "arbitrary")),
        name="gla",
    )(gla, small, w2p, b2, onorm, mbig, ind, indm)


def _head_cols(src, lane0, rows, width, nheads):
    lane_h = lax.broadcasted_iota(jnp.int32, (rows, nheads * width), 1) // width
    out = jnp.broadcast_to(src[:, lane0:lane0 + 1], (rows, nheads * width))
    for h in range(1, nheads):
        out = jnp.where(lane_h == h, jnp.broadcast_to(src[:, lane0 + h:lane0 + h + 1],
                                                      (rows, nheads * width)), out)
    return out


def _ssd_body(ssm_ref, small_ref, cw_ref, cb_ref, dtb_ref, alog_ref, dskip_ref, nw_ref, tri_ref,
              o_ref, prev_ref, st_ref):
    ts = ssm_ref.shape[0]
    L = SSM_CHUNK

    @pl.when(pl.program_id(1) == 0)
    def _():
        prev_ref[...] = jnp.zeros_like(prev_ref)
        st_ref[...] = jnp.zeros_like(st_ref)

    z = ssm_ref[:, 0:GROUP_WIDTH]
    xbc = ssm_ref[:, GROUP_WIDTH:GROUP_WIDTH + SSM_CONV_DIM]
    prev8 = prev_ref[...]
    row8 = lax.broadcasted_iota(jnp.int32, (8, SSM_CONV_DIM), 0)
    conv = cb_ref[...] + cw_ref[SSM_CONV - 1:SSM_CONV, :] * xbc
    for sft in range(1, SSM_CONV):
        rolled = pltpu.roll(xbc, sft, 0)
        top = jnp.where(row8 < sft, pltpu.roll(prev8, sft, 0), rolled[0:8])
        shifted = jnp.concatenate([top, rolled[8:]], axis=0)
        conv = conv + cw_ref[SSM_CONV - 1 - sft:SSM_CONV - sft, :] * shifted
    prev_ref[...] = xbc[ts - 8:ts]
    xc = _silu(conv)
    xs = xc[:, 0:GROUP_WIDTH]
    bm = xc[:, GROUP_WIDTH:GROUP_WIDTH + 2 * SSM_STATE]
    cm = xc[:, GROUP_WIDTH + 2 * SSM_STATE:]

    dt = _softplus(small_ref[...] + dtb_ref[...])
    a = -jnp.exp(alog_ref[...]) * dt
    xdt = xs * _head_cols(dt, SMALL_DT, ts, HEAD_DIM, HEADS)

    ii = lax.broadcasted_iota(jnp.int32, (L, L), 0)
    jj = lax.broadcasted_iota(jnp.int32, (L, L), 1)
    lane_lo = jj < HEAD_DIM
    row_lo = ii < HEAD_DIM
    tri = tri_ref[...]
    ys = []
    for c in range(ts // L):
        sl = slice(c * L, (c + 1) * L)
        cs = _exact_lhs_dot(tri, a[sl], wide=False)
        cs_t = cs.T
        cs_last = cs[L - 1:L, :]
        y_groups = []
        for grp in range(2):
            h0 = 2 * grp
            b_g = bm[sl, grp * SSM_STATE:(grp + 1) * SSM_STATE].astype(BF16)
            c_g = cm[sl, grp * SSM_STATE:(grp + 1) * SSM_STATE].astype(BF16)
            x_g = xdt[sl, grp * LANES:(grp + 1) * LANES]
            cb = _dot_nt(c_g, b_g)
            ws = []
            for h in (h0, h0 + 1):
                colh = cs[:, SMALL_DT + h:SMALL_DT + h + 1]
                rowh = cs_t[SMALL_DT + h:SMALL_DT + h + 1, :]
                ws.append(cb * jnp.exp(jnp.where(jj <= ii, colh - rowh, NEG)))
            yd = _dot(jnp.concatenate(ws, axis=0).astype(BF16), x_g.astype(BF16))
            y_diag = jnp.where(lane_lo, yd[0:L], yd[L:2 * L])
            cs_g = _head_cols(cs, SMALL_DT + h0, L, HEAD_DIM, 2)
            st_g = st_ref[grp * LANES:(grp + 1) * LANES, :]
            y_off = _dot_nt(c_g, st_g.astype(BF16)) * jnp.exp(cs_g)
            y_groups.append(y_diag + y_off)
            last_g = _head_cols(cs_last, SMALL_DT + h0, 1, HEAD_DIM, 2)
            x_dec = x_g * jnp.exp(last_g - cs_g)
            new = _dot_tn(x_dec.astype(BF16), b_g)
            e0 = jnp.exp(cs_last[:, SMALL_DT + h0:SMALL_DT + h0 + 1])
            e1 = jnp.exp(cs_last[:, SMALL_DT + h0 + 1:SMALL_DT + h0 + 2])
            st_ref[grp * LANES:(grp + 1) * LANES, :] = jnp.where(row_lo, e0, e1) * st_g + new
        ys.append(jnp.concatenate(y_groups, axis=1))
    y = jnp.concatenate(ys, axis=0)
    y = (y + dskip_ref[...] * xs) * _silu(z)
    halves = []
    for grp in range(2):
        yg = y[:, grp * LANES:(grp + 1) * LANES]
        halves.append(_rms(yg, nw_ref[:, grp * LANES:(grp + 1) * LANES]))
    o_ref[...] = jnp.concatenate(halves, axis=1).astype(o_ref.dtype)


def _ssd(ssm, small, cw, cb, dtb, alog, dskip, nw, b, s):
    ts = SCAN_ROWS
    ns = s // ts
    tri = jnp.asarray(np.tril(np.ones((SSM_CHUNK, SSM_CHUNK), np.float32)), BF16)
    row = lambda bi, i: (bi * ns + i, 0)
    full = lambda bi, i: (0, 0)
    return pl.pallas_call(
        _ssd_body,
        grid=(b, ns),
        in_specs=[pl.BlockSpec((ts, 1024), row),
                  pl.BlockSpec((ts, LANES), row),
                  pl.BlockSpec((SSM_CONV, SSM_CONV_DIM), full),
                  pl.BlockSpec((1, SSM_CONV_DIM), full),
                  pl.BlockSpec((1, LANES), full),
                  pl.BlockSpec((1, LANES), full),
                  pl.BlockSpec((1, GROUP_WIDTH), full),
                  pl.BlockSpec((1, GROUP_WIDTH), full),
                  pl.BlockSpec((SSM_CHUNK, SSM_CHUNK), full)],
        out_specs=pl.BlockSpec((ts, GROUP_WIDTH), row),
        out_shape=jax.ShapeDtypeStruct((b * s, GROUP_WIDTH), BF16),
        scratch_shapes=[pltpu.VMEM((8, SSM_CONV_DIM), F32), pltpu.VMEM((GROUP_WIDTH, SSM_STATE), F32)],
        compiler_params=_cparams(("parallel", "arbitrary")),
        name="ssd",
    )(ssm, small, cw, cb, dtb, alog, dskip, nw, tri)


def _out_ffn_body(x_ref, yaT_ref, yb_ref, ycT_ref, yd_ref, wo_ref, n2_ref, wg_ref, wu_ref, wd_ref,
                  fn_ref, o_ref, *, final_norm):
    tm = x_ref.shape[0]
    ya_t = yaT_ref[0, :, 0].reshape(GROUP_WIDTH, tm)
    yc_t = ycT_ref[0, :, 0].reshape(GROUP_WIDTH, tm)
    x1 = (x_ref[...]
          + _dot_tn(ya_t, wo_ref[0]) + _dot(yb_ref[...], wo_ref[1])
          + _dot_tn(yc_t, wo_ref[2]) + _dot(yd_ref[...], wo_ref[3]))
    h2 = _rms(x1, n2_ref[...]).astype(BF16)
    hidden = _silu(_dot(h2, wg_ref[...])) * _dot(h2, wu_ref[...])
    out = x1 + _dot(hidden.astype(BF16), wd_ref[...])
    if final_norm:
        out = _rms(out, fn_ref[...])
    o_ref[...] = out


def _out_ffn(x2, yaT, yb, ycT, yd, wo, n2, wg, wu, wd, fn, b, s, final_norm):
    tm = TILE_ROWS
    ns = s // tm
    row = lambda i: (i, 0)
    tr = lambda i: (i // ns, 0, i % ns, 0, 0)
    att_blk = (1, HEADS, 1, HEAD_DIM, tm)

    def resident(shape):
        return pl.BlockSpec(shape, lambda i: (0,) * len(shape), pipeline_mode=pl.Buffered(1))

    return pl.pallas_call(
        functools.partial(_out_ffn_body, final_norm=final_norm),
        grid=(b * ns,),
        in_specs=[pl.BlockSpec((tm, D_MODEL), row),
                  pl.BlockSpec(att_blk, tr),
                  pl.BlockSpec((tm, GROUP_WIDTH), row),
                  pl.BlockSpec(att_blk, tr),
                  pl.BlockSpec((tm, GROUP_WIDTH), row),
                  resident((4, GROUP_WIDTH, D_MODEL)),
                  resident((1, D_MODEL)),
                  resident((D_MODEL, FFN_HIDDEN)),
                  resident((D_MODEL, FFN_HIDDEN)),
                  resident((FFN_HIDDEN, D_MODEL)),
                  resident((1, D_MODEL))],
        out_specs=pl.BlockSpec((tm, D_MODEL), row),
        out_shape=jax.ShapeDtypeStruct((b * s, D_MODEL), F32),
        compiler_params=_cparams(("parallel",)),
        name="out_ffn",
    )(x2, yaT, yb, ycT, yd, wo, n2, wg, wu, wd, fn)


def _prep_w_in(w):
    z = lambda n: jnp.zeros((w.shape[0], n), w.dtype)
    small = jnp.concatenate([w[:, 768:772], w[:, 2996:3000], z(8), w[:, 1540:1556], z(32),
                             w[:, 1940:1972], z(32)], axis=1)
    out = jnp.concatenate([w[:, 0:768], w[:, 772:1540], w[:, 1556:1940], w[:, 1972:2996], small], axis=1)
    return out.astype(BF16)


def _lane_row(vals, lane0):
    return jnp.zeros((1, LANES), F32).at[0, lane0:lane0 + vals.shape[0]].set(vals.astype(F32))


def _prep_mla_w(w_uq, w_ukv):
    wq = w_uq.reshape(MLA_Q_LORA, HEADS, HEAD_DIM + MLA_ROPE)
    wq = jnp.pad(wq, ((0, 0), (0, 0), (0, LANES - HEAD_DIM - MLA_ROPE))).reshape(MLA_Q_LORA, HEADS * LANES)
    wkv = w_ukv.reshape(MLA_KV_LORA, HEADS, 2 * HEAD_DIM)
    wk = jnp.pad(wkv[:, :, :HEAD_DIM], ((0, 0), (0, 0), (0, LANES - HEAD_DIM))).reshape(MLA_KV_LORA, HEADS * LANES)
    wv = wkv[:, :, HEAD_DIM:].reshape(MLA_KV_LORA, GROUP_WIDTH)
    return wq.astype(BF16), wk.astype(BF16), wv.astype(BF16)


def kernel(x, positions, norm1, w_in, fox_f_bias, gla_gate_w2, gla_gate_b, gla_out_norm, mla_q_norm,
           mla_w_uq, mla_kv_norm, mla_w_ukv, ssm_conv_w, ssm_conv_b, ssm_dt_bias, ssm_A_log, ssm_D,
           ssm_norm, w_out, norm2, w_gate, w_up, w_down, final_norm):
    b, s, d = x.shape
    depth = w_in.shape[0]
    assert d == D_MODEL and s % TILE_ROWS == 0 and ATT_TQ == TILE_ROWS and ATT_TQ % ATT_TK == 0
    t = b * s
    x2 = x.reshape(t, d)

    half = MLA_ROPE // 2
    inv = ROPE_THETA ** (-jnp.arange(half, dtype=F32) / half)
    invf = jnp.tile(jnp.concatenate([inv, inv]), LANES // MLA_ROPE)[None, :]
    cos_t, sin_t = _rope_tables(positions.reshape(t, 1), invf)

    tri_att = jnp.asarray(np.tril(np.ones((CUMSUM_ROWS, CUMSUM_ROWS), np.float32)), BF16)

    for l in range(depth):
        foxq, foxkv, gla, mla, ssm, small = _inproj(x2, norm1[l][None, :], _prep_w_in(w_in[l]))

        qT, kA, vT = _fox_prep(foxq, foxkv, small, _lane_row(fox_f_bias[l], SMALL_FOX_F), tri_att, b, s)
        ya = _flash(qT, kA, vT)

        wq, wk, wv = _prep_mla_w(mla_w_uq[l], mla_w_ukv[l])
        qT, kA, vT = _mla_prep(mla, small, cos_t, sin_t, mla_q_norm[l][None, :], mla_kv_norm[l][None, :],
                               wq, wk, wv, b, s)
        yc = _flash(qT, kA, vT)

        w2p = jnp.zeros((LANES, HEADS * GLA_DK), F32).at[SMALL_GATE:SMALL_GATE + GLA_GATE_RANK].set(
            gla_gate_w2[l]).astype(BF16)
        yb = _gla(gla, small, w2p, gla_gate_b[l][None, :].astype(F32),
                  jnp.tile(gla_out_norm[l], HEADS)[None, :].astype(F32), b, s)

        yd = _ssd(ssm, small, ssm_conv_w[l], ssm_conv_b[l][None, :],
                  _lane_row(ssm_dt_bias[l], SMALL_DT), _lane_row(ssm_A_log[l], SMALL_DT),
                  jnp.repeat(ssm_D[l], HEAD_DIM)[None, :].astype(F32), ssm_norm[l][None, :], b, s)

        x2 = _out_ffn(x2, ya, yb, yc, yd, w_out[l].reshape(4, GROUP_WIDTH, D_MODEL).astype(BF16),
                      norm2[l][None, :], w_gate[l].astype(BF16), w_up[l].astype(BF16),
                      w_down[l].astype(BF16), final_norm[None, :], b, s,
                      final_norm=(l == depth - 1))
    return x2.reshape(b, s, d)
```

```python
import functools
import math

import numpy as np
import jax
import jax.numpy as jnp
from jax import lax
from jax.experimental import pallas as pl
from jax.experimental.pallas import tpu as pltpu

F32 = jnp.float32
BF16 = jnp.bfloat16

D_MODEL = 1024
GROUP_WIDTH = 256
HEADS = 4
HEAD_DIM = 64
GLA_DK = 32
GLA_GATE_RANK = 16
GLA_GATE_TAU = 16.0
GLA_CHUNK = 64
GLA_SUB = 8
MLA_ROPE = 32
MLA_Q_LORA = 256
MLA_KV_LORA = 128
ROPE_THETA = 10000.0
SSM_STATE = 128
SSM_CONV = 4
SSM_CHUNK = 128
SSM_CONV_DIM = 768
FFN_HIDDEN = 2816
EPS = 1e-6
NEG = -1e30
LOG2E = math.log2(math.e)
SUM_ROWS = 16

LANES = 128
QK_PAD = 128
VMEM_LIMIT = 56 * 1024 * 1024

SEG_FOXQ = (0, 256)
SEG_FOXKV = (256, 768)
SEG_GLA = (768, 1536)
SEG_MLA = (1536, 1920)
SEG_SSM = (1920, 2944)
SEG_SMALL = (2944, 3072)
IN_PAD = 3072
SMALL_FOX_F = 0
SMALL_DT = 4
SMALL_GATE = 16
SMALL_KROPE = 64

TILE_ROWS = 512
ATT_TQ = 512
ATT_TK = 256
FLASH_UNROLL = 4
SCAN_ROWS = 512
GLA_ROWS = 512
CUMSUM_ROWS = 256


def _cparams(sem):
    return pltpu.CompilerParams(dimension_semantics=sem, vmem_limit_bytes=VMEM_LIMIT)


def _rms(x, g):
    ms = jnp.mean(x * x, axis=-1, keepdims=True)
    return x * lax.rsqrt(ms + EPS) * g


def _log_sigmoid(x):
    return jnp.minimum(x, 0.0) - jnp.log1p(jnp.exp(-jnp.abs(x)))


def _softplus(x):
    return jnp.maximum(x, 0.0) + jnp.log1p(jnp.exp(-jnp.abs(x)))


def _silu(x):
    return x / (1.0 + jnp.exp(-x))


def _split3(x):
    hi = x.astype(BF16)
    r = x - hi.astype(F32)
    mid = r.astype(BF16)
    lo = (r - mid.astype(F32)).astype(BF16)
    return hi, mid, lo


def _dot(a, b):
    return jnp.dot(a, b, preferred_element_type=F32)


def _dot_nt(a, b):
    return lax.dot_general(a, b, (((1,), (1,)), ((), ())), preferred_element_type=F32)


def _dot_tn(a, b):
    return lax.dot_general(a, b, (((0,), (0,)), ((), ())), preferred_element_type=F32)


def _exact_lhs_dot(m01, x, wide=True):
    hi, mid, lo = _split3(x)
    if not wide:
        return _dot(m01, hi) + _dot(m01, mid) + _dot(m01, lo)
    w = x.shape[1]
    y = _dot(m01, jnp.concatenate([hi, mid, lo], axis=1))
    return y[:, 0:w] + y[:, w:2 * w] + y[:, 2 * w:3 * w]


def _inproj_body(x_ref, g_ref, w_ref, fb_ref, tri_ref, cos_ref, sin_ref, qn_ref, kvn_ref, wq_ref, wk_ref,
                 wv_ref, gla_ref, ssm_ref, small_ref, fqT_ref, fkA_ref, fvT_ref, mqT_ref, mkA_ref, mvT_ref,
                 carry_ref, *, tiles_per_seq):
    @pl.when(pl.program_id(0) % tiles_per_seq == 0)
    def _():
        carry_ref[...] = jnp.zeros_like(carry_ref)

    h = _rms(x_ref[...], g_ref[...]).astype(BF16)

    def seg(s):
        return _dot(h, w_ref[:, s[0]:s[1]])

    small = seg(SEG_SMALL)
    small_ref[...] = small
    _fox_prep_body(seg(SEG_FOXQ), seg(SEG_FOXKV).astype(BF16), small, fb_ref, tri_ref,
                   fqT_ref, fkA_ref, fvT_ref, carry_ref)
    _mla_prep_body(seg(SEG_MLA), small, cos_ref, sin_ref, qn_ref, kvn_ref, wq_ref, wk_ref, wv_ref,
                   mqT_ref, mkA_ref, mvT_ref)
    gla_ref[...] = seg(SEG_GLA)
    ssm_ref[...] = seg(SEG_SSM)


def _inproj(x2, g, w, fbias, tri, cos_t, sin_t, qn, kvn, wq, wk, wv, b, s):
    t = x2.shape[0]
    tm = TILE_ROWS
    ns = s // tm
    row = lambda i: (i, 0)
    full = lambda i: (0, 0)
    widths = [sg[1] - sg[0] for sg in (SEG_GLA, SEG_SSM, SEG_SMALL)]
    att_specs = [pl.BlockSpec((1, HEADS, 1, QK_PAD, tm), lambda i: (i // ns, 0, i % ns, 0, 0)),
                 pl.BlockSpec((1, HEADS, tm, QK_PAD), lambda i: (i // ns, 0, i % ns, 0)),
                 pl.BlockSpec((1, 1, tm // ATT_TK, GROUP_WIDTH, ATT_TK), lambda i: (i // ns, i % ns, 0, 0, 0))]
    att_shapes = [jax.ShapeDtypeStruct((b, HEADS, ns, QK_PAD, tm), BF16),
                  jax.ShapeDtypeStruct((b, HEADS, s, QK_PAD), BF16),
                  jax.ShapeDtypeStruct((b, ns, tm // ATT_TK, GROUP_WIDTH, ATT_TK), BF16)]
    return pl.pallas_call(
        functools.partial(_inproj_body, tiles_per_seq=ns),
        grid=(t // tm,),
        in_specs=[pl.BlockSpec((tm, D_MODEL), row),
                  pl.BlockSpec((1, D_MODEL), full),
                  pl.BlockSpec((D_MODEL, IN_PAD), full),
                  pl.BlockSpec((1, LANES), full),
                  pl.BlockSpec(tri.shape, full),
                  pl.BlockSpec((tm, LANES), row),
                  pl.BlockSpec((tm, LANES), row),
                  pl.BlockSpec((1, MLA_Q_LORA), full),
                  pl.BlockSpec((1, MLA_KV_LORA), full),
                  pl.BlockSpec((MLA_Q_LORA, HEADS * LANES), full),
                  pl.BlockSpec((MLA_KV_LORA, HEADS * LANES), full),
                  pl.BlockSpec((MLA_KV_LORA, GROUP_WIDTH), full)],
        out_specs=[pl.BlockSpec((tm, w_), row) for w_ in widths] + att_specs + att_specs,
        out_shape=[jax.ShapeDtypeStruct((t, w_), F32) for w_ in widths] + att_shapes + att_shapes,
        scratch_shapes=[pltpu.VMEM((1, LANES), F32)],
        compiler_params=_cparams(("arbitrary",)),
        name="inproj",
    )(x2, g, w, fbias, tri, cos_t, sin_t, qn, kvn, wq, wk, wv)


def _rope_table_body(pos_ref, invf_ref, cos_ref, sin_ref):
    tm = pos_ref.shape[0]
    groups = LANES // MLA_ROPE
    q = tm // groups
    lane = lax.broadcasted_iota(jnp.int32, (q, LANES), 1)
    pos = pos_ref[...].astype(F32)
    packed = jnp.zeros((q, LANES), F32)
    for m in range(groups):
        packed = jnp.where(lane // MLA_ROPE == m, jnp.broadcast_to(pos[m * q:(m + 1) * q], (q, LANES)), packed)
    ang = packed * invf_ref[...]
    cos_p, sin_p = jnp.cos(ang), jnp.sin(ang)
    inside = (lane >= 64) & (lane < 64 + MLA_ROPE)
    for m in range(groups):
        shift = (64 - MLA_ROPE * m) % LANES
        move = (lambda t: t) if shift == 0 else (lambda t: pltpu.roll(t, shift, 1))
        cos_ref[m * q:(m + 1) * q, :] = jnp.where(inside, move(cos_p), 1.0)
        sin_ref[m * q:(m + 1) * q, :] = jnp.where(inside, move(sin_p), 0.0)


def _rope_tables(pos_col, invf):
    t = pos_col.shape[0]
    tm = TILE_ROWS
    return pl.pallas_call(
        _rope_table_body,
        grid=(t // tm,),
        in_specs=[pl.BlockSpec((tm, 1), lambda i: (i, 0)),
                  pl.BlockSpec((1, LANES), lambda i: (0, 0))],
        out_specs=[pl.BlockSpec((tm, LANES), lambda i: (i, 0))] * 2,
        out_shape=[jax.ShapeDtypeStruct((t, LANES), F32)] * 2,
        compiler_params=_cparams(("parallel",)),
        name="rope_tables",
    )(pos_col, invf)


def _transpose_to_bf16(x):
    eye = (lax.broadcasted_iota(jnp.int32, (LANES, LANES), 0)
           == lax.broadcasted_iota(jnp.int32, (LANES, LANES), 1)).astype(BF16)
    return _dot_nt(eye, x.astype(BF16)).astype(BF16)


def _store_vT(vT_ref, v_all):
    for blk in range(v_all.shape[0] // ATT_TK):
        for pair in range(2):
            vp = v_all[blk * ATT_TK:(blk + 1) * ATT_TK, pair * LANES:(pair + 1) * LANES]
            vT_ref[0, 0, blk, pair * LANES:(pair + 1) * LANES, :] = _transpose_to_bf16(vp)


def _fox_prep_body(foxq_ref, foxkv_ref, small_ref, fb_ref, tri_ref, qT_ref, kA_ref, vT_ref, carry_ref):
    ts = foxq_ref.shape[0]

    logf = _log_sigmoid(small_ref[...] + fb_ref[...])
    blk = tri_ref.shape[0]
    pieces, run = [], carry_ref[...]
    for i0 in range(0, ts, blk):
        part = _exact_lhs_dot(tri_ref[...], logf[i0:i0 + blk]) + run
        run = part[blk - 1:blk, :]
        pieces.append(part)
    fcum = jnp.concatenate(pieces, axis=0)
    carry_ref[...] = run
    f_hi, f_mid, f_lo = [p.astype(F32) for p in _split3(fcum * LOG2E)]

    lane = lax.broadcasted_iota(jnp.int32, (ts, LANES), 1)
    low = lane < HEAD_DIM
    packed = jnp.where(lane < HEADS, f_hi,
             jnp.where(lane < 2 * HEADS, pltpu.roll(f_mid, HEADS, 1), pltpu.roll(f_lo, 2 * HEADS, 1)))
    f_q = pltpu.roll(packed, HEAD_DIM, 1)
    f_k = -pltpu.roll(packed, HEAD_DIM + 3 * HEADS, 1)
    _store_vT(vT_ref, foxkv_ref[:, 256:512].astype(F32))
    for pair in range(2):
        qp = foxq_ref[:, pair * LANES:(pair + 1) * LANES] * (LOG2E * HEAD_DIM ** -0.5)
        kp = foxkv_ref[:, pair * LANES:(pair + 1) * LANES].astype(F32)
        for e in range(2):
            h = 2 * pair + e
            qh = qp if e == 0 else pltpu.roll(qp, HEAD_DIM, 1)
            kh = kp if e == 0 else pltpu.roll(kp, HEAD_DIM, 1)
            mine = (lane - HEAD_DIM) % HEADS == h
            first = mine & (lane >= HEAD_DIM) & (lane < HEAD_DIM + 3 * HEADS)
            second = mine & (lane >= HEAD_DIM + 3 * HEADS) & (lane < HEAD_DIM + 6 * HEADS)
            qa = jnp.where(low, qh, jnp.where(first, f_q, jnp.where(second, 1.0, 0.0)))
            ka = jnp.where(low, kh, jnp.where(first, 1.0, jnp.where(second, f_k, 0.0)))
            qT_ref[0, h, 0] = _transpose_to_bf16(qa)
            kA_ref[0, h] = ka.astype(BF16)


def _mla_prep_body(mla_ref, small_ref, cos_ref, sin_ref, qn_ref, kvn_ref, wq_ref, wk_ref, wv_ref,
                   qT_ref, kA_ref, vT_ref):
    ts = mla_ref.shape[0]
    lane = lax.broadcasted_iota(jnp.int32, (ts, LANES), 1)
    cosv = cos_ref[...]
    sinv = sin_ref[...]
    half = MLA_ROPE // 2
    sin_a = jnp.where((lane >= 64) & (lane < 64 + half), -sinv, 0.0)
    sin_b = jnp.where((lane >= 64 + half) & (lane < 64 + MLA_ROPE), sinv, 0.0)

    def rope(x):
        return (x * cosv + pltpu.roll(x, LANES - half, 1) * sin_a + pltpu.roll(x, half, 1) * sin_b)

    cq = _rms(mla_ref[:, 0:MLA_Q_LORA], qn_ref[...]).astype(BF16)
    ckv = _rms(mla_ref[:, MLA_Q_LORA:MLA_Q_LORA + MLA_KV_LORA], kvn_ref[...]).astype(BF16)
    kr = rope(jnp.where((lane >= 64) & (lane < 64 + MLA_ROPE), small_ref[...], 0.0))
    q_all = _dot(cq, wq_ref[...]) * (LOG2E * (HEAD_DIM + MLA_ROPE) ** -0.5)
    k_all = _dot(ckv, wk_ref[...])
    for h in range(HEADS):
        qh = rope(q_all[:, h * LANES:(h + 1) * LANES])
        qT_ref[0, h, 0] = _transpose_to_bf16(qh)
        kA_ref[0, h] = (k_all[:, h * LANES:(h + 1) * LANES] + kr).astype(BF16)
    _store_vT(vT_ref, _dot(ckv, wv_ref[...]))


def _flash_body(qT_ref, k_ref, vT_ref, o_ref, m_ref, acc_ref, s_ref, p_ref, a_ref, mb_ref, *, tq, tk):
    nq = qT_ref.shape[2]
    nb = tq // tk
    assert nb % 2 == 0
    ones = jnp.ones((SUM_ROWS, tk), BF16)

    def score_block(qi, j, slot, lo=0, with_max=True):
        start = pl.multiple_of(j * tk, tk)
        s = _dot(k_ref[0, 0, pl.ds(start, tk), :], qT_ref[0, 0, qi, :, lo:])
        s_ref[slot, :, lo:] = s
        if with_max:
            mb_ref[slot] = jnp.max(s, axis=0, keepdims=True)

    score_block(0, 0, 0)

    def q_tile(i, carry):
        m_ref[...] = jnp.full_like(m_ref, NEG)
        acc_ref[...] = jnp.zeros_like(acc_ref)
        p_ref[1] = jnp.zeros_like(p_ref[1])
        a_ref[1] = jnp.ones_like(a_ref[1])
        n_full = nb * i

        def scores(j, slot, lo=0, with_max=True):
            score_block(i, j, slot, lo, with_max)

        def softmax(slot, lo=0, diag=False):
            s = s_ref[slot, :, lo:]
            if diag:
                keep = (lax.broadcasted_iota(jnp.int32, (tk, tk), 0)
                        <= lax.broadcasted_iota(jnp.int32, (tk, tk), 1))
                left = jnp.where(keep, s[:, :tk], NEG)
                s = left if s.shape[1] == tk else jnp.concatenate([left, s[:, tk:]], axis=1)
                m_blk = jnp.max(s, axis=0, keepdims=True)
            else:
                m_blk = mb_ref[slot]
            m_prev = m_ref[:, lo:]
            m_new = jnp.maximum(m_prev, m_blk)
            a_ref[slot, :, lo:] = jnp.exp2(m_prev - m_new)
            p_ref[slot, :, lo:] = jnp.exp2(s - m_new).astype(BF16)
            m_ref[:, lo:] = m_new

        def pv(j, slot, lo=0):
            v_ext = jnp.concatenate([vT_ref[0, j, 0], ones], axis=0)
            acc_ref[:, lo:] = (a_ref[slot, :, lo:] * acc_ref[:, lo:]
                               + _dot(v_ext, p_ref[slot, :, lo:]))

        def run_blocks(j, count):
            for u in range(count):
                scores(j + u + 1, (u + 1) % 2)
                softmax(u % 2)
                pv(jnp.maximum(j + u - 1, 0), (u + 1) % 2)

        per_trip = nb * FLASH_UNROLL
        n_trips = i // FLASH_UNROLL

        def body(t, c2):
            run_blocks(per_trip * t, per_trip)
            return c2

        lax.fori_loop(0, n_trips, body, 0)
        done = per_trip * n_trips
        part = FLASH_UNROLL // 2
        while part >= 1:
            take = (i % (2 * part)) >= part

            @pl.when(take)
            def _(done=done, part=part):
                run_blocks(done, nb * part)

            done = done + jnp.where(take, nb * part, 0)
            part //= 2
        for u in range(nb):
            if u + 1 < nb:
                scores(n_full + u + 1, (u + 1) % 2, lo=(u + 1) * tk, with_max=False)
            softmax(u % 2, lo=u * tk, diag=True)
            pv(jnp.maximum(n_full + u - 1, 0), (u + 1) % 2, lo=max(u - 1, 0) * tk)
        score_block(jnp.minimum(i + 1, nq - 1), 0, 0)
        pv(n_full + nb - 1, (nb - 1) % 2, lo=(nb - 1) * tk)
        o_ref[0, 0, i] = (acc_ref[0:HEAD_DIM, :] / acc_ref[HEAD_DIM:HEAD_DIM + 1, :]).astype(o_ref.dtype)
        return carry

    lax.fori_loop(0, nq, q_tile, 0)


def _flash(qT, kA, vT):
    b, _, nq, _, tq = qT.shape
    tk = ATT_TK
    s = nq * tq
    nk = s // tk
    vT5 = vT.reshape(b, nk, HEADS, HEAD_DIM, tk)
    return pl.pallas_call(
        functools.partial(_flash_body, tq=tq, tk=tk),
        grid=(b, HEADS),
        in_specs=[pl.BlockSpec((1, 1, nq, QK_PAD, tq), lambda bi, h: (bi, h, 0, 0, 0)),
                  pl.BlockSpec((1, 1, s, QK_PAD), lambda bi, h: (bi, h, 0, 0)),
                  pl.BlockSpec((1, nk, 1, HEAD_DIM, tk), lambda bi, h: (bi, 0, h, 0, 0))],
        out_specs=pl.BlockSpec((1, 1, nq, HEAD_DIM, tq), lambda bi, h: (bi, h, 0, 0, 0)),
        out_shape=jax.ShapeDtypeStruct((b, HEADS, nq, HEAD_DIM, tq), BF16),
        scratch_shapes=[pltpu.VMEM((1, tq), F32),
                        pltpu.VMEM((HEAD_DIM + SUM_ROWS, tq), F32),
                        pltpu.VMEM((2, tk, tq), F32),
                        pltpu.VMEM((2, tk, tq), BF16),
                        pltpu.VMEM((2, 1, tq), F32),
                        pltpu.VMEM((2, 1, tq), F32)],
        compiler_params=_cparams(("parallel", "parallel")),
        name="flash",
    )(qT, kA, vT5)


def _gla_consts(ts):
    idx = np.arange(ts)
    same = (idx[:, None] // GLA_CHUNK) == (idx[None, :] // GLA_CHUNK)
    mbig = (same & (idx[None, :] <= idx[:, None])).astype(np.float32)
    hd = np.arange(HEADS * GLA_DK) // GLA_DK
    hv = np.arange(GROUP_WIDTH) // HEAD_DIM
    ind_dk_dv = (hd[:, None] == hv[None, :]).astype(np.float32)
    ind_mean = (hv[:, None] == hv[None, :]).astype(np.float32) / HEAD_DIM
    return (jnp.asarray(mbig, BF16), jnp.asarray(ind_dk_dv, BF16), jnp.asarray(ind_mean, BF16))


def _pick_heads(stacked, rows):
    lane_h = lax.broadcasted_iota(jnp.int32, (rows, GROUP_WIDTH), 1) // HEAD_DIM
    out = jnp.zeros((rows, GROUP_WIDTH), F32)
    for h in range(HEADS):
        out = jnp.where(lane_h == h, stacked[h * rows:(h + 1) * rows, :], out)
    return out


def _gla_body(gla_ref, small_ref, w2_ref, b2_ref, on_ref, mbig_ref, ind_ref, indm_ref, o_ref, st_ref,
              gpad_ref, kpad_ref, vpad_ref):
    ts = gla_ref.shape[0]
    L = GLA_CHUNK
    nsub = L // GLA_SUB
    PAD = GLA_SUB

    @pl.when(pl.program_id(1) == 0)
    def _():
        st_ref[...] = jnp.zeros_like(st_ref)

    q = gla_ref[:, 0:128] * (GLA_DK ** -0.5)
    k = gla_ref[:, 128:256]
    v = gla_ref[:, 256:512]
    r = gla_ref[:, 512:768]
    x = _dot(small_ref[...].astype(BF16), w2_ref[...]) + b2_ref[...]
    g = _log_sigmoid(x) * (1.0 / GLA_GATE_TAU)
    cb = mbig_ref.shape[0]
    G = jnp.concatenate([_exact_lhs_dot(mbig_ref[...], g[i0:i0 + cb]) for i0 in range(0, ts, cb)], axis=0)
    r_rows, gl_rows = [], []
    for c in range(ts // L):
        for I in range(nsub):
            at = c * L + I * GLA_SUB
            r_rows.append(jnp.zeros((GLA_SUB, 128), F32) if I == 0
                          else jnp.broadcast_to(G[at - 1:at, :], (GLA_SUB, 128)))
        gl_rows.append(jnp.broadcast_to(G[(c + 1) * L - 1:(c + 1) * L, :], (L, 128)))
    R = jnp.concatenate(r_rows, axis=0)
    GL = jnp.concatenate(gl_rows, axis=0)
    q_inter = q * jnp.exp(G)
    q_norm = q * jnp.exp(G - R)
    k_dec = k * jnp.exp(GL - G)
    for ref, val in ((gpad_ref, G), (kpad_ref, k), (vpad_ref, v)):
        ref[0:PAD, :] = jnp.zeros((PAD, val.shape[1]), F32)
        ref[PAD:PAD + ts, :] = val

    lane_h = lax.broadcasted_iota(jnp.int32, (L, 128), 1) // GLA_DK
    row = lax.broadcasted_iota(jnp.int32, (L, 128), 0)
    sub_pos = row % GLA_SUB
    sub_idx = row // GLA_SUB
    row_blk = lax.broadcasted_iota(jnp.int32, (L, L * 2), 0) // GLA_SUB
    zeros_v = jnp.zeros((L, GROUP_WIDTH), BF16)

    def stack_heads(a):
        return jnp.concatenate([jnp.where(lane_h == h, a, 0.0) for h in range(HEADS)], axis=0)

    outs = []
    for c in range(ts // L):
        sl = slice(c * L, (c + 1) * L)
        Gc, kc, vc, qc = G[sl], k[sl], v[sl], q[sl]
        vc16 = vc.astype(BF16)

        r_next = jnp.concatenate([R[c * L + GLA_SUB:(c + 1) * L], R[(c + 1) * L - GLA_SUB:(c + 1) * L]],
                                 axis=0)
        base = kc * jnp.exp(jnp.where(sub_idx < nsub - 1, r_next - Gc, NEG))
        kts = []
        kt = jnp.zeros_like(base)
        for I in range(1, nsub):
            if I > 1:
                r_i = R[c * L + I * GLA_SUB:c * L + I * GLA_SUB + 1, :]
                r_p = R[c * L + (I - 1) * GLA_SUB:c * L + (I - 1) * GLA_SUB + 1, :]
                kt = kt * jnp.exp(r_i - r_p)
            kt = jnp.where(sub_idx == I - 1, base, kt)
            kts.append(kt)
        kts.append(jnp.zeros_like(base))
        kstack = jnp.concatenate(kts, axis=0).astype(BF16)
        sc = _dot_nt(stack_heads(q_norm[sl]).astype(BF16), kstack)
        a_rows = []
        for h in range(HEADS):
            sh = sc[h * L:(h + 1) * L]
            a_h = jnp.zeros((L, 2 * L), F32)
            for tile in range(nsub // 2):
                mine = (row_blk == 2 * tile + 1) | (row_blk == 2 * tile + 2)
                a_h = jnp.where(mine, sh[:, tile * 2 * L:(tile + 1) * 2 * L], a_h)
            a_rows.append(jnp.where(row_blk % 2 == 0, pltpu.roll(a_h, L, 1), a_h))
        a_st = jnp.concatenate(a_rows, axis=0).astype(BF16)
        v_pad = jnp.concatenate([vc16, zeros_v], axis=0)
        st = st_ref[...]
        big = _dot(a_st, v_pad) + _dot_nt(stack_heads(q_inter[sl]).astype(BF16), st.astype(BF16))
        o_c = _pick_heads(big, L)

        ps = []
        for d in range(GLA_SUB):
            if d == 0:
                p = qc * kc
            else:
                lo = PAD + c * L - d
                arg = jnp.where(sub_pos >= d, Gc - gpad_ref[lo:lo + L, :], NEG)
                p = qc * kpad_ref[lo:lo + L, :] * jnp.exp(arg)
            ps.append(p)
        abc = _dot(jnp.concatenate(ps, axis=0).astype(BF16), ind_ref[...])
        for d in range(GLA_SUB):
            lo = PAD + c * L - d
            o_c = o_c + abc[d * L:(d + 1) * L] * vpad_ref[lo:lo + L, :]
        outs.append(o_c)

        st_ref[...] = (jnp.exp(GL[c * L:c * L + 1, :]) * st
                       + _dot_tn(vc16, k_dec[sl].astype(BF16)))

    o = jnp.concatenate(outs, axis=0)
    o2 = o * o
    hi = o2.astype(BF16)
    lo = (o2 - hi.astype(F32)).astype(BF16)
    ms = _dot(hi, indm_ref[...]) + _dot(lo, indm_ref[...])
    o_ref[...] = (o * lax.rsqrt(ms + EPS) * on_ref[...] * _silu(r)).astype(o_ref.dtype)


def _gla(gla, small, w2p, b2, onorm, b, s):
    ts = GLA_ROWS
    ns = s // ts
    mbig, ind, indm = _gla_consts(CUMSUM_ROWS)
    row = lambda bi, i: (bi * ns + i, 0)
    full = lambda bi, i: (0, 0)
    return pl.pallas_call(
        _gla_body,
        grid=(b, ns),
        in_specs=[pl.BlockSpec((ts, 768), row),
                  pl.BlockSpec((ts, LANES), row),
                  pl.BlockSpec((LANES, 128), full),
                  pl.BlockSpec((1, 128), full),
                  pl.BlockSpec((1, GROUP_WIDTH), full),
                  pl.BlockSpec(mbig.shape, full),
                  pl.BlockSpec((128, GROUP_WIDTH), full),
                  pl.BlockSpec((GROUP_WIDTH, GROUP_WIDTH), full)],
        out_specs=pl.BlockSpec((ts, GROUP_WIDTH), row),
        out_shape=jax.ShapeDtypeStruct((b * s, GROUP_WIDTH), BF16),
        scratch_shapes=[pltpu.VMEM((GROUP_WIDTH, 128), F32),
                        pltpu.VMEM((GLA_SUB + ts, 128), F32),
                        pltpu.VMEM((GLA_SUB + ts, 128), F32),
                        pltpu.VMEM((GLA_SUB + ts, GROUP_WIDTH), F32)],
        compiler_params=_cparams(("parallel", "arbitrary")),
        name="gla",
    )(gla, small, w2p, b2, onorm, mbig, ind, indm)


def _head_cols(src, lane0, rows, width, nheads):
    lane_h = lax.broadcasted_iota(jnp.int32, (rows, nheads * width), 1) // width
    out = jnp.broadcast_to(src[:, lane0:lane0 + 1], (rows, nheads * width))
    for h in range(1, nheads):
        out = jnp.where(lane_h == h, jnp.broadcast_to(src[:, lane0 + h:lane0 + h + 1],
                                                      (rows, nheads * width)), out)
    return out


def _ssd_body(ssm_ref, small_ref, cw_ref, cb_ref, dtb_ref, alog_ref, dskip_ref, nw_ref, tri_ref,
              o_ref, prev_ref, st_ref):
    ts = ssm_ref.shape[0]
    L = SSM_CHUNK

    @pl.when(pl.program_id(1) == 0)
    def _():
        prev_ref[...] = jnp.zeros_like(prev_ref)
        st_ref[...] = jnp.zeros_like(st_ref)

    z = ssm_ref[:, 0:GROUP_WIDTH]
    xbc = ssm_ref[:, GROUP_WIDTH:GROUP_WIDTH + SSM_CONV_DIM]
    prev8 = prev_ref[...]
    row8 = lax.broadcasted_iota(jnp.int32, (8, SSM_CONV_DIM), 0)
    conv = cb_ref[...] + cw_ref[SSM_CONV - 1:SSM_CONV, :] * xbc
    for sft in range(1, SSM_CONV):
        rolled = pltpu.roll(xbc, sft, 0)
        top = jnp.where(row8 < sft, pltpu.roll(prev8, sft, 0), rolled[0:8])
        shifted = jnp.concatenate([top, rolled[8:]], axis=0)
        conv = conv + cw_ref[SSM_CONV - 1 - sft:SSM_CONV - sft, :] * shifted
    prev_ref[...] = xbc[ts - 8:ts]
    xc = _silu(conv)
    xs = xc[:, 0:GROUP_WIDTH]
    bm = xc[:, GROUP_WIDTH:GROUP_WIDTH + 2 * SSM_STATE]
    cm = xc[:, GROUP_WIDTH + 2 * SSM_STATE:]

    dt = _softplus(small_ref[...] + dtb_ref[...])
    a = -jnp.exp(alog_ref[...]) * dt
    xdt = xs * _head_cols(dt, SMALL_DT, ts, HEAD_DIM, HEADS)

    ii = lax.broadcasted_iota(jnp.int32, (L, L), 0)
    jj = lax.broadcasted_iota(jnp.int32, (L, L), 1)
    lane_lo = jj < HEAD_DIM
    row_lo = ii < HEAD_DIM
    tri = tri_ref[...]
    ys = []
    for c in range(ts // L):
        sl = slice(c * L, (c + 1) * L)
        cs = _exact_lhs_dot(tri, a[sl], wide=False)
        cs_t = cs.T
        cs_last = cs[L - 1:L, :]
        y_groups = []
        for grp in range(2):
            h0 = 2 * grp
            b_g = bm[sl, grp * SSM_STATE:(grp + 1) * SSM_STATE].astype(BF16)
            c_g = cm[sl, grp * SSM_STATE:(grp + 1) * SSM_STATE].astype(BF16)
            x_g = xdt[sl, grp * LANES:(grp + 1) * LANES]
            cb = _dot_nt(c_g, b_g)
            ws = []
            for h in (h0, h0 + 1):
                colh = cs[:, SMALL_DT + h:SMALL_DT + h + 1]
                rowh = cs_t[SMALL_DT + h:SMALL_DT + h + 1, :]
                ws.append(cb * jnp.exp(jnp.where(jj <= ii, colh - rowh, NEG)))
            yd = _dot(jnp.concatenate(ws, axis=0).astype(BF16), x_g.astype(BF16))
            y_diag = jnp.where(lane_lo, yd[0:L], yd[L:2 * L])
            cs_g = _head_cols(cs, SMALL_DT + h0, L, HEAD_DIM, 2)
            st_g = st_ref[grp * LANES:(grp + 1) * LANES, :]
            y_off = _dot_nt(c_g, st_g.astype(BF16)) * jnp.exp(cs_g)
            y_groups.append(y_diag + y_off)
            last_g = _head_cols(cs_last, SMALL_DT + h0, 1, HEAD_DIM, 2)
            x_dec = x_g * jnp.exp(last_g - cs_g)
            new = _dot_tn(x_dec.astype(BF16), b_g)
            e0 = jnp.exp(cs_last[:, SMALL_DT + h0:SMALL_DT + h0 + 1])
            e1 = jnp.exp(cs_last[:, SMALL_DT + h0 + 1:SMALL_DT + h0 + 2])
            st_ref[grp * LANES:(grp + 1) * LANES, :] = jnp.where(row_lo, e0, e1) * st_g + new
        ys.append(jnp.concatenate(y_groups, axis=1))
    y = jnp.concatenate(ys, axis=0)
    y = (y + dskip_ref[...] * xs) * _silu(z)
    halves = []
    for grp in range(2):
        yg = y[:, grp * LANES:(grp + 1) * LANES]
        halves.append(_rms(yg, nw_ref[:, grp * LANES:(grp + 1) * LANES]))
    o_ref[...] = jnp.concatenate(halves, axis=1).astype(o_ref.dtype)


def _ssd(ssm, small, cw, cb, dtb, alog, dskip, nw, b, s):
    ts = SCAN_ROWS
    ns = s // ts
    tri = jnp.asarray(np.tril(np.ones((SSM_CHUNK, SSM_CHUNK), np.float32)), BF16)
    row = lambda bi, i: (bi * ns + i, 0)
    full = lambda bi, i: (0, 0)
    return pl.pallas_call(
        _ssd_body,
        grid=(b, ns),
        in_specs=[pl.BlockSpec((ts, 1024), row),
                  pl.BlockSpec((ts, LANES), row),
                  pl.BlockSpec((SSM_CONV, SSM_CONV_DIM), full),
                  pl.BlockSpec((1, SSM_CONV_DIM), full),
                  pl.BlockSpec((1, LANES), full),
                  pl.BlockSpec((1, LANES), full),
                  pl.BlockSpec((1, GROUP_WIDTH), full),
                  pl.BlockSpec((1, GROUP_WIDTH), full),
                  pl.BlockSpec((SSM_CHUNK, SSM_CHUNK), full)],
        out_specs=pl.BlockSpec((ts, GROUP_WIDTH), row),
        out_shape=jax.ShapeDtypeStruct((b * s, GROUP_WIDTH), BF16),
        scratch_shapes=[pltpu.VMEM((8, SSM_CONV_DIM), F32), pltpu.VMEM((GROUP_WIDTH, SSM_STATE), F32)],
        compiler_params=_cparams(("parallel", "arbitrary")),
        name="ssd",
    )(ssm, small, cw, cb, dtb, alog, dskip, nw, tri)


def _out_ffn_body(x_ref, yaT_ref, yb_ref, ycT_ref, yd_ref, wo_ref, n2_ref, wg_ref, wu_ref, wd_ref,
                  fn_ref, o_ref, *, final_norm):
    tm = x_ref.shape[0]
    ya_t = yaT_ref[0, :, 0].reshape(GROUP_WIDTH, tm)
    yc_t = ycT_ref[0, :, 0].reshape(GROUP_WIDTH, tm)
    x1 = (x_ref[...]
          + _dot_tn(ya_t, wo_ref[0]) + _dot(yb_ref[...], wo_ref[1])
          + _dot_tn(yc_t, wo_ref[2]) + _dot(yd_ref[...], wo_ref[3]))
    h2 = _rms(x1, n2_ref[...]).astype(BF16)
    hidden = _silu(_dot(h2, wg_ref[...])) * _dot(h2, wu_ref[...])
    out = x1 + _dot(hidden.astype(BF16), wd_ref[...])
    if final_norm:
        out = _rms(out, fn_ref[...])
    o_ref[...] = out


def _out_ffn(x2, yaT, yb, ycT, yd, wo, n2, wg, wu, wd, fn, b, s, final_norm):
    tm = TILE_ROWS
    ns = s // tm
    row = lambda i: (i, 0)
    tr = lambda i: (i // ns, 0, i % ns, 0, 0)
    att_blk = (1, HEADS, 1, HEAD_DIM, tm)

    def resident(shape):
        return pl.BlockSpec(shape, lambda i: (0,) * len(shape), pipeline_mode=pl.Buffered(1))

    return pl.pallas_call(
        functools.partial(_out_ffn_body, final_norm=final_norm),
        grid=(b * ns,),
        in_specs=[pl.BlockSpec((tm, D_MODEL), row),
                  pl.BlockSpec(att_blk, tr),
                  pl.BlockSpec((tm, GROUP_WIDTH), row),
                  pl.BlockSpec(att_blk, tr),
                  pl.BlockSpec((tm, GROUP_WIDTH), row),
                  resident((4, GROUP_WIDTH, D_MODEL)),
                  resident((1, D_MODEL)),
                  resident((D_MODEL, FFN_HIDDEN)),
                  resident((D_MODEL, FFN_HIDDEN)),
                  resident((FFN_HIDDEN, D_MODEL)),
                  resident((1, D_MODEL))],
        out_specs=pl.BlockSpec((tm, D_MODEL), row),
        out_shape=jax.ShapeDtypeStruct((b * s, D_MODEL), F32),
        compiler_params=_cparams(("parallel",)),
        name="out_ffn",
    )(x2, yaT, yb, ycT, yd, wo, n2, wg, wu, wd, fn)


def _prep_w_in(w):
    z = lambda n: jnp.zeros((w.shape[0], n), w.dtype)
    small = jnp.concatenate([w[:, 768:772], w[:, 2996:3000], z(8), w[:, 1540:1556], z(32),
                             w[:, 1940:1972], z(32)], axis=1)
    out = jnp.concatenate([w[:, 0:768], w[:, 772:1540], w[:, 1556:1940], w[:, 1972:2996], small], axis=1)
    return out.astype(BF16)


def _lane_row(vals, lane0):
    return jnp.zeros((1, LANES), F32).at[0, lane0:lane0 + vals.shape[0]].set(vals.astype(F32))


def _prep_mla_w(w_uq, w_ukv):
    wq = w_uq.reshape(MLA_Q_LORA, HEADS, HEAD_DIM + MLA_ROPE)
    wq = jnp.pad(wq, ((0, 0), (0, 0), (0, LANES - HEAD_DIM - MLA_ROPE))).reshape(MLA_Q_LORA, HEADS * LANES)
    wkv = w_ukv.reshape(MLA_KV_LORA, HEADS, 2 * HEAD_DIM)
    wk = jnp.pad(wkv[:, :, :HEAD_DIM], ((0, 0), (0, 0), (0, LANES - HEAD_DIM))).reshape(MLA_KV_LORA, HEADS * LANES)
    wv = wkv[:, :, HEAD_DIM:].reshape(MLA_KV_LORA, GROUP_WIDTH)
    return wq.astype(BF16), wk.astype(BF16), wv.astype(BF16)


def kernel(x, positions, norm1, w_in, fox_f_bias, gla_gate_w2, gla_gate_b, gla_out_norm, mla_q_norm,
           mla_w_uq, mla_kv_norm, mla_w_ukv, ssm_conv_w, ssm_conv_b, ssm_dt_bias, ssm_A_log, ssm_D,
           ssm_norm, w_out, norm2, w_gate, w_up, w_down, final_norm):
    b, s, d = x.shape
    depth = w_in.shape[0]
    assert d == D_MODEL and s % TILE_ROWS == 0 and ATT_TQ == TILE_ROWS and ATT_TQ % ATT_TK == 0
    t = b * s
    x2 = x.reshape(t, d)

    half = MLA_ROPE // 2
    inv = ROPE_THETA ** (-jnp.arange(half, dtype=F32) / half)
    invf = jnp.tile(jnp.concatenate([inv, inv]), LANES // MLA_ROPE)[None, :]
    cos_t, sin_t = _rope_tables(positions.reshape(t, 1), invf)

    tri_att = jnp.asarray(np.tril(np.ones((CUMSUM_ROWS, CUMSUM_ROWS), np.float32)), BF16)

    for l in range(depth):
        wq, wk, wv = _prep_mla_w(mla_w_uq[l], mla_w_ukv[l])
        gla, ssm, small, fqT, fkA, fvT, mqT, mkA, mvT = _inproj(
            x2, norm1[l][None, :], _prep_w_in(w_in[l]), _lane_row(fox_f_bias[l], SMALL_FOX_F), tri_att,
            cos_t, sin_t, mla_q_norm[l][None, :], mla_kv_norm[l][None, :], wq, wk, wv, b, s)
        ya = _flash(fqT, fkA, fvT)
        yc = _flash(mqT, mkA, mvT)

        w2p = jnp.zeros((LANES, HEADS * GLA_DK), F32).at[SMALL_GATE:SMALL_GATE + GLA_GATE_RANK].set(
            gla_gate_w2[l]).astype(BF16)
        yb = _gla(gla, small, w2p, gla_gate_b[l][None, :].astype(F32),
                  jnp.tile(gla_out_norm[l], HEADS)[None, :].astype(F32), b, s)

        yd = _ssd(ssm, small, ssm_conv_w[l], ssm_conv_b[l][None, :],
                  _lane_row(ssm_dt_bias[l], SMALL_DT), _lane_row(ssm_A_log[l], SMALL_DT),
                  jnp.repeat(ssm_D[l], HEAD_DIM)[None, :].astype(F32), ssm_norm[l][None, :], b, s)

        x2 = _out_ffn(x2, ya, yb, yc, yd, w_out[l].reshape(4, GROUP_WIDTH, D_MODEL).astype(BF16),
                      norm2[l][None, :], w_gate[l].astype(BF16), w_up[l].astype(BF16),
                      w_down[l].astype(BF16), final_norm[None, :], b, s,
                      final_norm=(l == depth - 1))
    return x2.reshape(b, s, d)
```

```python
import functools
import math

import numpy as np
import jax
import jax.numpy as jnp
from jax import lax
from jax.experimental import pallas as pl
from jax.experimental.pallas import tpu as pltpu

F32 = jnp.float32
BF16 = jnp.bfloat16

D_MODEL = 1024
GROUP_WIDTH = 256
HEADS = 4
HEAD_DIM = 64
GLA_DK = 32
GLA_GATE_RANK = 16
GLA_GATE_TAU = 16.0
GLA_CHUNK = 64
GLA_SUB = 8
MLA_ROPE = 32
MLA_Q_LORA = 256
MLA_KV_LORA = 128
ROPE_THETA = 10000.0
SSM_STATE = 128
SSM_CONV = 4
SSM_CHUNK = 128
SSM_CONV_DIM = 768
FFN_HIDDEN = 2816
EPS = 1e-6
NEG = -1e30
LOG2E = math.log2(math.e)
SUM_ROWS = 16

LANES = 128
QK_PAD = 128
VMEM_LIMIT = 56 * 1024 * 1024

SEG_FOXQ = (0, 256)
SEG_FOXKV = (256, 768)
SEG_GLA = (768, 1536)
SEG_MLA = (1536, 1920)
SEG_SSM = (1920, 2944)
SEG_SMALL = (2944, 3072)
IN_PAD = 3072
SMALL_FOX_F = 0
SMALL_DT = 4
SMALL_GATE = 16
SMALL_KROPE = 64

TILE_ROWS = 512
ATT_TQ = 512
ATT_TK = 256
FLASH_UNROLL = 4
SCAN_ROWS = 512
GLA_ROWS = 512
CUMSUM_ROWS = 256


def _cparams(sem):
    return pltpu.CompilerParams(dimension_semantics=sem, vmem_limit_bytes=VMEM_LIMIT)


def _rms(x, g):
    ms = jnp.mean(x * x, axis=-1, keepdims=True)
    return x * lax.rsqrt(ms + EPS) * g


def _log_sigmoid(x):
    return jnp.minimum(x, 0.0) - jnp.log1p(jnp.exp(-jnp.abs(x)))


def _softplus(x):
    return jnp.maximum(x, 0.0) + jnp.log1p(jnp.exp(-jnp.abs(x)))


def _silu(x):
    return x / (1.0 + jnp.exp(-x))


def _split3(x):
    hi = x.astype(BF16)
    r = x - hi.astype(F32)
    mid = r.astype(BF16)
    lo = (r - mid.astype(F32)).astype(BF16)
    return hi, mid, lo


def _dot(a, b):
    return jnp.dot(a, b, preferred_element_type=F32)


def _dot_nt(a, b):
    return lax.dot_general(a, b, (((1,), (1,)), ((), ())), preferred_element_type=F32)


def _dot_tn(a, b):
    return lax.dot_general(a, b, (((0,), (0,)), ((), ())), preferred_element_type=F32)


def _exact_lhs_dot(m01, x, wide=True):
    hi, mid, lo = _split3(x)
    if not wide:
        return _dot(m01, hi) + _dot(m01, mid) + _dot(m01, lo)
    w = x.shape[1]
    y = _dot(m01, jnp.concatenate([hi, mid, lo], axis=1))
    return y[:, 0:w] + y[:, w:2 * w] + y[:, 2 * w:3 * w]


def _inproj_body(x_ref, g_ref, w_ref, fb_ref, tri_ref, cos_ref, sin_ref, qn_ref, kvn_ref, wq_ref, wk_ref,
                 wv_ref, gla_ref, ssm_ref, small_ref, fqT_ref, fkA_ref, fvT_ref, mqT_ref, mkA_ref, mvT_ref,
                 carry_ref, *, tiles_per_seq):
    @pl.when(pl.program_id(0) % tiles_per_seq == 0)
    def _():
        carry_ref[...] = jnp.zeros_like(carry_ref)

    h = _rms(x_ref[...], g_ref[...]).astype(BF16)

    def seg(s):
        return _dot(h, w_ref[:, s[0]:s[1]])

    small = seg(SEG_SMALL)
    small_ref[...] = small
    foxq, foxkv, mla = seg(SEG_FOXQ), seg(SEG_FOXKV).astype(BF16), seg(SEG_MLA)
    gla_ref[...] = seg(SEG_GLA)
    _fox_prep_body(foxq, foxkv, small, fb_ref, tri_ref, fqT_ref, fkA_ref, fvT_ref, carry_ref)
    ssm_ref[...] = seg(SEG_SSM)
    _mla_prep_body(mla, small, cos_ref, sin_ref, qn_ref, kvn_ref, wq_ref, wk_ref, wv_ref,
                   mqT_ref, mkA_ref, mvT_ref)


def _inproj(x2, g, w, fbias, tri, cos_t, sin_t, qn, kvn, wq, wk, wv, b, s):
    t = x2.shape[0]
    tm = TILE_ROWS
    ns = s // tm
    row = lambda i: (i, 0)
    full = lambda i: (0, 0)
    widths = [sg[1] - sg[0] for sg in (SEG_GLA, SEG_SSM, SEG_SMALL)]
    att_specs = [pl.BlockSpec((1, HEADS, 1, QK_PAD, tm), lambda i: (i // ns, 0, i % ns, 0, 0)),
                 pl.BlockSpec((1, HEADS, tm, QK_PAD), lambda i: (i // ns, 0, i % ns, 0)),
                 pl.BlockSpec((1, 1, tm // ATT_TK, GROUP_WIDTH, ATT_TK), lambda i: (i // ns, i % ns, 0, 0, 0))]
    att_shapes = [jax.ShapeDtypeStruct((b, HEADS, ns, QK_PAD, tm), BF16),
                  jax.ShapeDtypeStruct((b, HEADS, s, QK_PAD), BF16),
                  jax.ShapeDtypeStruct((b, ns, tm // ATT_TK, GROUP_WIDTH, ATT_TK), BF16)]
    return pl.pallas_call(
        functools.partial(_inproj_body, tiles_per_seq=ns),
        grid=(t // tm,),
        in_specs=[pl.BlockSpec((tm, D_MODEL), row),
                  pl.BlockSpec((1, D_MODEL), full),
                  pl.BlockSpec((D_MODEL, IN_PAD), full),
                  pl.BlockSpec((1, LANES), full),
                  pl.BlockSpec(tri.shape, full),
                  pl.BlockSpec((tm, LANES), row),
                  pl.BlockSpec((tm, LANES), row),
                  pl.BlockSpec((1, MLA_Q_LORA), full),
                  pl.BlockSpec((1, MLA_KV_LORA), full),
                  pl.BlockSpec((MLA_Q_LORA, HEADS * LANES), full),
                  pl.BlockSpec((MLA_KV_LORA, HEADS * LANES), full),
                  pl.BlockSpec((MLA_KV_LORA, GROUP_WIDTH), full)],
        out_specs=[pl.BlockSpec((tm, w_), row) for w_ in widths] + att_specs + att_specs,
        out_shape=[jax.ShapeDtypeStruct((t, w_), F32) for w_ in widths] + att_shapes + att_shapes,
        scratch_shapes=[pltpu.VMEM((1, LANES), F32)],
        compiler_params=_cparams(("arbitrary",)),
        name="inproj",
    )(x2, g, w, fbias, tri, cos_t, sin_t, qn, kvn, wq, wk, wv)


def _rope_table_body(pos_ref, invf_ref, cos_ref, sin_ref):
    tm = pos_ref.shape[0]
    groups = LANES // MLA_ROPE
    q = tm // groups
    lane = lax.broadcasted_iota(jnp.int32, (q, LANES), 1)
    pos = pos_ref[...].astype(F32)
    packed = jnp.zeros((q, LANES), F32)
    for m in range(groups):
        packed = jnp.where(lane // MLA_ROPE == m, jnp.broadcast_to(pos[m * q:(m + 1) * q], (q, LANES)), packed)
    ang = packed * invf_ref[...]
    cos_p, sin_p = jnp.cos(ang), jnp.sin(ang)
    inside = (lane >= 64) & (lane < 64 + MLA_ROPE)
    for m in range(groups):
        shift = (64 - MLA_ROPE * m) % LANES
        move = (lambda t: t) if shift == 0 else (lambda t: pltpu.roll(t, shift, 1))
        cos_ref[m * q:(m + 1) * q, :] = jnp.where(inside, move(cos_p), 1.0)
        sin_ref[m * q:(m + 1) * q, :] = jnp.where(inside, move(sin_p), 0.0)


def _rope_tables(pos_col, invf):
    t = pos_col.shape[0]
    tm = TILE_ROWS
    return pl.pallas_call(
        _rope_table_body,
        grid=(t // tm,),
        in_specs=[pl.BlockSpec((tm, 1), lambda i: (i, 0)),
                  pl.BlockSpec((1, LANES), lambda i: (0, 0))],
        out_specs=[pl.BlockSpec((tm, LANES), lambda i: (i, 0))] * 2,
        out_shape=[jax.ShapeDtypeStruct((t, LANES), F32)] * 2,
        compiler_params=_cparams(("parallel",)),
        name="rope_tables",
    )(pos_col, invf)


def _transpose_to_bf16(x):
    eye = (lax.broadcasted_iota(jnp.int32, (LANES, LANES), 0)
           == lax.broadcasted_iota(jnp.int32, (LANES, LANES), 1)).astype(BF16)
    return _dot_nt(eye, x.astype(BF16)).astype(BF16)


def _store_vT(vT_ref, v_all):
    for blk in range(v_all.shape[0] // ATT_TK):
        for pair in range(2):
            vp = v_all[blk * ATT_TK:(blk + 1) * ATT_TK, pair * LANES:(pair + 1) * LANES]
            vT_ref[0, 0, blk, pair * LANES:(pair + 1) * LANES, :] = _transpose_to_bf16(vp)


def _fox_prep_body(foxq_ref, foxkv_ref, small_ref, fb_ref, tri_ref, qT_ref, kA_ref, vT_ref, carry_ref):
    ts = foxq_ref.shape[0]

    logf = _log_sigmoid(small_ref[...] + fb_ref[...])
    blk = tri_ref.shape[0]
    pieces, run = [], carry_ref[...]
    for i0 in range(0, ts, blk):
        part = _exact_lhs_dot(tri_ref[...], logf[i0:i0 + blk]) + run
        run = part[blk - 1:blk, :]
        pieces.append(part)
    fcum = jnp.concatenate(pieces, axis=0)
    carry_ref[...] = run
    f_hi, f_mid, f_lo = [p.astype(F32) for p in _split3(fcum * LOG2E)]

    lane = lax.broadcasted_iota(jnp.int32, (ts, LANES), 1)
    low = lane < HEAD_DIM
    packed = jnp.where(lane < HEADS, f_hi,
             jnp.where(lane < 2 * HEADS, pltpu.roll(f_mid, HEADS, 1), pltpu.roll(f_lo, 2 * HEADS, 1)))
    f_q = pltpu.roll(packed, HEAD_DIM, 1)
    f_k = -pltpu.roll(packed, HEAD_DIM + 3 * HEADS, 1)
    _store_vT(vT_ref, foxkv_ref[:, 256:512].astype(F32))
    for pair in range(2):
        qp = foxq_ref[:, pair * LANES:(pair + 1) * LANES] * (LOG2E * HEAD_DIM ** -0.5)
        kp = foxkv_ref[:, pair * LANES:(pair + 1) * LANES].astype(F32)
        for e in range(2):
            h = 2 * pair + e
            qh = qp if e == 0 else pltpu.roll(qp, HEAD_DIM, 1)
            kh = kp if e == 0 else pltpu.roll(kp, HEAD_DIM, 1)
            mine = (lane - HEAD_DIM) % HEADS == h
            first = mine & (lane >= HEAD_DIM) & (lane < HEAD_DIM + 3 * HEADS)
            second = mine & (lane >= HEAD_DIM + 3 * HEADS) & (lane < HEAD_DIM + 6 * HEADS)
            qa = jnp.where(low, qh, jnp.where(first, f_q, jnp.where(second, 1.0, 0.0)))
            ka = jnp.where(low, kh, jnp.where(first, 1.0, jnp.where(second, f_k, 0.0)))
            qT_ref[0, h, 0] = _transpose_to_bf16(qa)
            kA_ref[0, h] = ka.astype(BF16)


def _mla_prep_body(mla_ref, small_ref, cos_ref, sin_ref, qn_ref, kvn_ref, wq_ref, wk_ref, wv_ref,
                   qT_ref, kA_ref, vT_ref):
    ts = mla_ref.shape[0]
    lane = lax.broadcasted_iota(jnp.int32, (ts, LANES), 1)
    cosv = cos_ref[...]
    sinv = sin_ref[...]
    half = MLA_ROPE // 2
    sin_a = jnp.where((lane >= 64) & (lane < 64 + half), -sinv, 0.0)
    sin_b = jnp.where((lane >= 64 + half) & (lane < 64 + MLA_ROPE), sinv, 0.0)

    def rope(x):
        return (x * cosv + pltpu.roll(x, LANES - half, 1) * sin_a + pltpu.roll(x, half, 1) * sin_b)

    cq = _rms(mla_ref[:, 0:MLA_Q_LORA], qn_ref[...]).astype(BF16)
    ckv = _rms(mla_ref[:, MLA_Q_LORA:MLA_Q_LORA + MLA_KV_LORA], kvn_ref[...]).astype(BF16)
    kr = rope(jnp.where((lane >= 64) & (lane < 64 + MLA_ROPE), small_ref[...], 0.0))
    q_all = _dot(cq, wq_ref[...]) * (LOG2E * (HEAD_DIM + MLA_ROPE) ** -0.5)
    k_all = _dot(ckv, wk_ref[...])
    for h in range(HEADS):
        qh = rope(q_all[:, h * LANES:(h + 1) * LANES])
        qT_ref[0, h, 0] = _transpose_to_bf16(qh)
        kA_ref[0, h] = (k_all[:, h * LANES:(h + 1) * LANES] + kr).astype(BF16)
    _store_vT(vT_ref, _dot(ckv, wv_ref[...]))


def _flash_body(qT_ref, k_ref, vT_ref, o_ref, m_ref, acc_ref, s_ref, p_ref, a_ref, mb_ref, *, tq, tk):
    nq = qT_ref.shape[2]
    nb = tq // tk
    assert nb % 2 == 0
    ones = jnp.ones((SUM_ROWS, tk), BF16)

    def score_block(qi, j, slot, lo=0, with_max=True):
        start = pl.multiple_of(j * tk, tk)
        s = _dot(k_ref[0, 0, pl.ds(start, tk), :], qT_ref[0, 0, qi, :, lo:])
        s_ref[slot, :, lo:] = s
        if with_max:
            mb_ref[slot] = jnp.max(s, axis=0, keepdims=True)

    score_block(0, 0, 0)

    def q_tile(i, carry):
        m_ref[...] = jnp.full_like(m_ref, NEG)
        acc_ref[...] = jnp.zeros_like(acc_ref)
        p_ref[1] = jnp.zeros_like(p_ref[1])
        a_ref[1] = jnp.ones_like(a_ref[1])
        n_full = nb * i

        def scores(j, slot, lo=0, with_max=True):
            score_block(i, j, slot, lo, with_max)

        def softmax(slot, lo=0, diag=False):
            s = s_ref[slot, :, lo:]
            if diag:
                keep = (lax.broadcasted_iota(jnp.int32, (tk, tk), 0)
                        <= lax.broadcasted_iota(jnp.int32, (tk, tk), 1))
                left = jnp.where(keep, s[:, :tk], NEG)
                s = left if s.shape[1] == tk else jnp.concatenate([left, s[:, tk:]], axis=1)
                m_blk = jnp.max(s, axis=0, keepdims=True)
            else:
                m_blk = mb_ref[slot]
            m_prev = m_ref[:, lo:]
            m_new = jnp.maximum(m_prev, m_blk)
            a_ref[slot, :, lo:] = jnp.exp2(m_prev - m_new)
            p_ref[slot, :, lo:] = jnp.exp2(s - m_new).astype(BF16)
            m_ref[:, lo:] = m_new

        def pv(j, slot, lo=0):
            v_ext = jnp.concatenate([vT_ref[0, j, 0], ones], axis=0)
            acc_ref[:, lo:] = (a_ref[slot, :, lo:] * acc_ref[:, lo:]
                               + _dot(v_ext, p_ref[slot, :, lo:]))

        def run_blocks(j, count):
            for u in range(count):
                scores(j + u + 1, (u + 1) % 2)
                softmax(u % 2)
                pv(jnp.maximum(j + u - 1, 0), (u + 1) % 2)

        per_trip = nb * FLASH_UNROLL
        n_trips = i // FLASH_UNROLL

        def body(t, c2):
            run_blocks(per_trip * t, per_trip)
            return c2

        lax.fori_loop(0, n_trips, body, 0)
        done = per_trip * n_trips
        part = FLASH_UNROLL // 2
        while part >= 1:
            take = (i % (2 * part)) >= part

            @pl.when(take)
            def _(done=done, part=part):
                run_blocks(done, nb * part)

            done = done + jnp.where(take, nb * part, 0)
            part //= 2
        for u in range(nb):
            if u + 1 < nb:
                scores(n_full + u + 1, (u + 1) % 2, lo=(u + 1) * tk, with_max=False)
            softmax(u % 2, lo=u * tk, diag=True)
            pv(jnp.maximum(n_full + u - 1, 0), (u + 1) % 2, lo=max(u - 1, 0) * tk)
        score_block(jnp.minimum(i + 1, nq - 1), 0, 0)
        pv(n_full + nb - 1, (nb - 1) % 2, lo=(nb - 1) * tk)
        o_ref[0, 0, i] = (acc_ref[0:HEAD_DIM, :] / acc_ref[HEAD_DIM:HEAD_DIM + 1, :]).astype(o_ref.dtype)
        return carry

    lax.fori_loop(0, nq, q_tile, 0)


def _flash(qT, kA, vT):
    b, _, nq, _, tq = qT.shape
    tk = ATT_TK
    s = nq * tq
    nk = s // tk
    vT5 = vT.reshape(b, nk, HEADS, HEAD_DIM, tk)
    return pl.pallas_call(
        functools.partial(_flash_body, tq=tq, tk=tk),
        grid=(b, HEADS),
        in_specs=[pl.BlockSpec((1, 1, nq, QK_PAD, tq), lambda bi, h: (bi, h, 0, 0, 0)),
                  pl.BlockSpec((1, 1, s, QK_PAD), lambda bi, h: (bi, h, 0, 0)),
                  pl.BlockSpec((1, nk, 1, HEAD_DIM, tk), lambda bi, h: (bi, 0, h, 0, 0))],
        out_specs=pl.BlockSpec((1, 1, nq, HEAD_DIM, tq), lambda bi, h: (bi, h, 0, 0, 0)),
        out_shape=jax.ShapeDtypeStruct((b, HEADS, nq, HEAD_DIM, tq), BF16),
        scratch_shapes=[pltpu.VMEM((1, tq), F32),
                        pltpu.VMEM((HEAD_DIM + SUM_ROWS, tq), F32),
                        pltpu.VMEM((2, tk, tq), F32),
                        pltpu.VMEM((2, tk, tq), BF16),
                        pltpu.VMEM((2, 1, tq), F32),
                        pltpu.VMEM((2, 1, tq), F32)],
        compiler_params=_cparams(("parallel", "parallel")),
        name="flash",
    )(qT, kA, vT5)


def _gla_consts(ts):
    idx = np.arange(ts)
    same = (idx[:, None] // GLA_CHUNK) == (idx[None, :] // GLA_CHUNK)
    mbig = (same & (idx[None, :] <= idx[:, None])).astype(np.float32)
    hd = np.arange(HEADS * GLA_DK) // GLA_DK
    hv = np.arange(GROUP_WIDTH) // HEAD_DIM
    ind_dk_dv = (hd[:, None] == hv[None, :]).astype(np.float32)
    ind_mean = (hv[:, None] == hv[None, :]).astype(np.float32) / HEAD_DIM
    return (jnp.asarray(mbig, BF16), jnp.asarray(ind_dk_dv, BF16), jnp.asarray(ind_mean, BF16))


def _pick_heads(stacked, rows):
    lane_h = lax.broadcasted_iota(jnp.int32, (rows, GROUP_WIDTH), 1) // HEAD_DIM
    out = jnp.zeros((rows, GROUP_WIDTH), F32)
    for h in range(HEADS):
        out = jnp.where(lane_h == h, stacked[h * rows:(h + 1) * rows, :], out)
    return out


def _gla_body(gla_ref, small_ref, w2_ref, b2_ref, on_ref, mbig_ref, ind_ref, indm_ref, o_ref, st_ref,
              gpad_ref, kpad_ref, vpad_ref):
    ts = gla_ref.shape[0]
    L = GLA_CHUNK
    nsub = L // GLA_SUB
    PAD = GLA_SUB

    @pl.when(pl.program_id(1) == 0)
    def _():
        st_ref[...] = jnp.zeros_like(st_ref)

    q = gla_ref[:, 0:128] * (GLA_DK ** -0.5)
    k = gla_ref[:, 128:256]
    v = gla_ref[:, 256:512]
    r = gla_ref[:, 512:768]
    x = _dot(small_ref[...].astype(BF16), w2_ref[...]) + b2_ref[...]
    g = _log_sigmoid(x) * (1.0 / GLA_GATE_TAU)
    cb = mbig_ref.shape[0]
    G = jnp.concatenate([_exact_lhs_dot(mbig_ref[...], g[i0:i0 + cb]) for i0 in range(0, ts, cb)], axis=0)
    r_rows, gl_rows = [], []
    for c in range(ts // L):
        for I in range(nsub):
            at = c * L + I * GLA_SUB
            r_rows.append(jnp.zeros((GLA_SUB, 128), F32) if I == 0
                          else jnp.broadcast_to(G[at - 1:at, :], (GLA_SUB, 128)))
        gl_rows.append(jnp.broadcast_to(G[(c + 1) * L - 1:(c + 1) * L, :], (L, 128)))
    R = jnp.concatenate(r_rows, axis=0)
    GL = jnp.concatenate(gl_rows, axis=0)
    q_inter = q * jnp.exp(G)
    q_norm = q * jnp.exp(G - R)
    k_dec = k * jnp.exp(GL - G)
    for ref, val in ((gpad_ref, G), (kpad_ref, k), (vpad_ref, v)):
        ref[0:PAD, :] = jnp.zeros((PAD, val.shape[1]), F32)
        ref[PAD:PAD + ts, :] = val

    lane_h = lax.broadcasted_iota(jnp.int32, (L, 128), 1) // GLA_DK
    row = lax.broadcasted_iota(jnp.int32, (L, 128), 0)
    sub_pos = row % GLA_SUB
    sub_idx = row // GLA_SUB
    row_blk = lax.broadcasted_iota(jnp.int32, (L, L * 2), 0) // GLA_SUB
    zeros_v = jnp.zeros((L, GROUP_WIDTH), BF16)

    def stack_heads(a):
        return jnp.concatenate([jnp.where(lane_h == h, a, 0.0) for h in range(HEADS)], axis=0)

    outs = []
    for c in range(ts // L):
        sl = slice(c * L, (c + 1) * L)
        Gc, kc, vc, qc = G[sl], k[sl], v[sl], q[sl]
        vc16 = vc.astype(BF16)

        r_next = jnp.concatenate([R[c * L + GLA_SUB:(c + 1) * L], R[(c + 1) * L - GLA_SUB:(c + 1) * L]],
                                 axis=0)
        base = kc * jnp.exp(jnp.where(sub_idx < nsub - 1, r_next - Gc, NEG))
        kts = []
        kt = jnp.zeros_like(base)
        for I in range(1, nsub):
            if I > 1:
                r_i = R[c * L + I * GLA_SUB:c * L + I * GLA_SUB + 1, :]
                r_p = R[c * L + (I - 1) * GLA_SUB:c * L + (I - 1) * GLA_SUB + 1, :]
                kt = kt * jnp.exp(r_i - r_p)
            kt = jnp.where(sub_idx == I - 1, base, kt)
            kts.append(kt)
        kts.append(jnp.zeros_like(base))
        kstack = jnp.concatenate(kts, axis=0).astype(BF16)
        sc = _dot_nt(stack_heads(q_norm[sl]).astype(BF16), kstack)
        a_rows = []
        for h in range(HEADS):
            sh = sc[h * L:(h + 1) * L]
            a_h = jnp.zeros((L, 2 * L), F32)
            for tile in range(nsub // 2):
                mine = (row_blk == 2 * tile + 1) | (row_blk == 2 * tile + 2)
                a_h = jnp.where(mine, sh[:, tile * 2 * L:(tile + 1) * 2 * L], a_h)
            a_rows.append(jnp.where(row_blk % 2 == 0, pltpu.roll(a_h, L, 1), a_h))
        a_st = jnp.concatenate(a_rows, axis=0).astype(BF16)
        v_pad = jnp.concatenate([vc16, zeros_v], axis=0)
        st = st_ref[...]
        big = _dot(a_st, v_pad) + _dot_nt(stack_heads(q_inter[sl]).astype(BF16), st.astype(BF16))
        o_c = _pick_heads(big, L)

        ps = []
        for d in range(GLA_SUB):
            if d == 0:
                p = qc * kc
            else:
                lo = PAD + c * L - d
                arg = jnp.where(sub_pos >= d, Gc - gpad_ref[lo:lo + L, :], NEG)
                p = qc * kpad_ref[lo:lo + L, :] * jnp.exp(arg)
            ps.append(p)
        abc = _dot(jnp.concatenate(ps, axis=0).astype(BF16), ind_ref[...])
        for d in range(GLA_SUB):
            lo = PAD + c * L - d
            o_c = o_c + abc[d * L:(d + 1) * L] * vpad_ref[lo:lo + L, :]
        outs.append(o_c)

        st_ref[...] = (jnp.exp(GL[c * L:c * L + 1, :]) * st
                       + _dot_tn(vc16, k_dec[sl].astype(BF16)))

    o = jnp.concatenate(outs, axis=0)
    o2 = o * o
    hi = o2.astype(BF16)
    lo = (o2 - hi.astype(F32)).astype(BF16)
    ms = _dot(hi, indm_ref[...]) + _dot(lo, indm_ref[...])
    o_ref[...] = (o * lax.rsqrt(ms + EPS) * on_ref[...] * _silu(r)).astype(o_ref.dtype)


def _gla(gla, small, w2p, b2, onorm, b, s):
    ts = GLA_ROWS
    ns = s // ts
    mbig, ind, indm = _gla_consts(CUMSUM_ROWS)
    row = lambda bi, i: (bi * ns + i, 0)
    full = lambda bi, i: (0, 0)
    return pl.pallas_call(
        _gla_body,
        grid=(b, ns),
        in_specs=[pl.BlockSpec((ts, 768), row),
                  pl.BlockSpec((ts, LANES), row),
                  pl.BlockSpec((LANES, 128), full),
                  pl.BlockSpec((1, 128), full),
                  pl.BlockSpec((1, GROUP_WIDTH), full),
                  pl.BlockSpec(mbig.shape, full),
                  pl.BlockSpec((128, GROUP_WIDTH), full),
                  pl.BlockSpec((GROUP_WIDTH, GROUP_WIDTH), full)],
        out_specs=pl.BlockSpec((ts, GROUP_WIDTH), row),
        out_shape=jax.ShapeDtypeStruct((b * s, GROUP_WIDTH), BF16),
        scratch_shapes=[pltpu.VMEM((GROUP_WIDTH, 128), F32),
                        pltpu.VMEM((GLA_SUB + ts, 128), F32),
                        pltpu.VMEM((GLA_SUB + ts, 128), F32),
                        pltpu.VMEM((GLA_SUB + ts, GROUP_WIDTH), F32)],
        compiler_params=_cparams(("parallel", "arbitrary")),
        name="gla",
    )(gla, small, w2p, b2, onorm, mbig, ind, indm)


def _head_cols(src, lane0, rows, width, nheads):
    lane_h = lax.broadcasted_iota(jnp.int32, (rows, nheads * width), 1) // width
    out = jnp.broadcast_to(src[:, lane0:lane0 + 1], (rows, nheads * width))
    for h in range(1, nheads):
        out = jnp.where(lane_h == h, jnp.broadcast_to(src[:, lane0 + h:lane0 + h + 1],
                                                      (rows, nheads * width)), out)
    return out


def _ssd_body(ssm_ref, small_ref, cw_ref, cb_ref, dtb_ref, alog_ref, dskip_ref, nw_ref, tri_ref,
              o_ref, prev_ref, st_ref):
    ts = ssm_ref.shape[0]
    L = SSM_CHUNK

    @pl.when(pl.program_id(1) == 0)
    def _():
        prev_ref[...] = jnp.zeros_like(prev_ref)
        st_ref[...] = jnp.zeros_like(st_ref)

    z = ssm_ref[:, 0:GROUP_WIDTH]
    xbc = ssm_ref[:, GROUP_WIDTH:GROUP_WIDTH + SSM_CONV_DIM]
    prev8 = prev_ref[...]
    row8 = lax.broadcasted_iota(jnp.int32, (8, SSM_CONV_DIM), 0)
    conv = cb_ref[...] + cw_ref[SSM_CONV - 1:SSM_CONV, :] * xbc
    for sft in range(1, SSM_CONV):
        rolled = pltpu.roll(xbc, sft, 0)
        top = jnp.where(row8 < sft, pltpu.roll(prev8, sft, 0), rolled[0:8])
        shifted = jnp.concatenate([top, rolled[8:]], axis=0)
        conv = conv + cw_ref[SSM_CONV - 1 - sft:SSM_CONV - sft, :] * shifted
    prev_ref[...] = xbc[ts - 8:ts]
    xc = _silu(conv)
    xs = xc[:, 0:GROUP_WIDTH]
    bm = xc[:, GROUP_WIDTH:GROUP_WIDTH + 2 * SSM_STATE]
    cm = xc[:, GROUP_WIDTH + 2 * SSM_STATE:]

    dt = _softplus(small_ref[...] + dtb_ref[...])
    a = -jnp.exp(alog_ref[...]) * dt
    xdt = xs * _head_cols(dt, SMALL_DT, ts, HEAD_DIM, HEADS)

    ii = lax.broadcasted_iota(jnp.int32, (L, L), 0)
    jj = lax.broadcasted_iota(jnp.int32, (L, L), 1)
    lane_lo = jj < HEAD_DIM
    row_lo = ii < HEAD_DIM
    tri = tri_ref[...]
    ys = []
    for c in range(ts // L):
        sl = slice(c * L, (c + 1) * L)
        cs = _exact_lhs_dot(tri, a[sl], wide=False)
        cs_t = cs.T
        cs_last = cs[L - 1:L, :]
        y_groups = []
        for grp in range(2):
            h0 = 2 * grp
            b_g = bm[sl, grp * SSM_STATE:(grp + 1) * SSM_STATE].astype(BF16)
            c_g = cm[sl, grp * SSM_STATE:(grp + 1) * SSM_STATE].astype(BF16)
            x_g = xdt[sl, grp * LANES:(grp + 1) * LANES]
            cb = _dot_nt(c_g, b_g)
            ws = []
            for h in (h0, h0 + 1):
                colh = cs[:, SMALL_DT + h:SMALL_DT + h + 1]
                rowh = cs_t[SMALL_DT + h:SMALL_DT + h + 1, :]
                ws.append(cb * jnp.exp(jnp.where(jj <= ii, colh - rowh, NEG)))
            yd = _dot(jnp.concatenate(ws, axis=0).astype(BF16), x_g.astype(BF16))
            y_diag = jnp.where(lane_lo, yd[0:L], yd[L:2 * L])
            cs_g = _head_cols(cs, SMALL_DT + h0, L, HEAD_DIM, 2)
            st_g = st_ref[grp * LANES:(grp + 1) * LANES, :]
            y_off = _dot_nt(c_g, st_g.astype(BF16)) * jnp.exp(cs_g)
            y_groups.append(y_diag + y_off)
            last_g = _head_cols(cs_last, SMALL_DT + h0, 1, HEAD_DIM, 2)
            x_dec = x_g * jnp.exp(last_g - cs_g)
            new = _dot_tn(x_dec.astype(BF16), b_g)
            e0 = jnp.exp(cs_last[:, SMALL_DT + h0:SMALL_DT + h0 + 1])
            e1 = jnp.exp(cs_last[:, SMALL_DT + h0 + 1:SMALL_DT + h0 + 2])
            st_ref[grp * LANES:(grp + 1) * LANES, :] = jnp.where(row_lo, e0, e1) * st_g + new
        ys.append(jnp.concatenate(y_groups, axis=1))
    y = jnp.concatenate(ys, axis=0)
    y = (y + dskip_ref[...] * xs) * _silu(z)
    halves = []
    for grp in range(2):
        yg = y[:, grp * LANES:(grp + 1) * LANES]
        halves.append(_rms(yg, nw_ref[:, grp * LANES:(grp + 1) * LANES]))
    o_ref[...] = jnp.concatenate(halves, axis=1).astype(o_ref.dtype)


def _ssd(ssm, small, cw, cb, dtb, alog, dskip, nw, b, s):
    ts = SCAN_ROWS
    ns = s // ts
    tri = jnp.asarray(np.tril(np.ones((SSM_CHUNK, SSM_CHUNK), np.float32)), BF16)
    row = lambda bi, i: (bi * ns + i, 0)
    full = lambda bi, i: (0, 0)
    return pl.pallas_call(
        _ssd_body,
        grid=(b, ns),
        in_specs=[pl.BlockSpec((ts, 1024), row),
                  pl.BlockSpec((ts, LANES), row),
                  pl.BlockSpec((SSM_CONV, SSM_CONV_DIM), full),
                  pl.BlockSpec((1, SSM_CONV_DIM), full),
                  pl.BlockSpec((1, LANES), full),
                  pl.BlockSpec((1, LANES), full),
                  pl.BlockSpec((1, GROUP_WIDTH), full),
                  pl.BlockSpec((1, GROUP_WIDTH), full),
                  pl.BlockSpec((SSM_CHUNK, SSM_CHUNK), full)],
        out_specs=pl.BlockSpec((ts, GROUP_WIDTH), row),
        out_shape=jax.ShapeDtypeStruct((b * s, GROUP_WIDTH), BF16),
        scratch_shapes=[pltpu.VMEM((8, SSM_CONV_DIM), F32), pltpu.VMEM((GROUP_WIDTH, SSM_STATE), F32)],
        compiler_params=_cparams(("parallel", "arbitrary")),
        name="ssd",
    )(ssm, small, cw, cb, dtb, alog, dskip, nw, tri)


def _out_ffn_body(x_ref, yaT_ref, yb_ref, ycT_ref, yd_ref, wo_ref, n2_ref, wg_ref, wu_ref, wd_ref,
                  fn_ref, o_ref, *, final_norm):
    tm = x_ref.shape[0]
    ya_t = yaT_ref[0, :, 0].reshape(GROUP_WIDTH, tm)
    yc_t = ycT_ref[0, :, 0].reshape(GROUP_WIDTH, tm)
    x1 = (x_ref[...]
          + _dot_tn(ya_t, wo_ref[0]) + _dot(yb_ref[...], wo_ref[1])
          + _dot_tn(yc_t, wo_ref[2]) + _dot(yd_ref[...], wo_ref[3]))
    h2 = _rms(x1, n2_ref[...]).astype(BF16)
    hidden = _silu(_dot(h2, wg_ref[...])) * _dot(h2, wu_ref[...])
    out = x1 + _dot(hidden.astype(BF16), wd_ref[...])
    if final_norm:
        out = _rms(out, fn_ref[...])
    o_ref[...] = out


def _out_ffn(x2, yaT, yb, ycT, yd, wo, n2, wg, wu, wd, fn, b, s, final_norm):
    tm = TILE_ROWS
    ns = s // tm
    row = lambda i: (i, 0)
    tr = lambda i: (i // ns, 0, i % ns, 0, 0)
    att_blk = (1, HEADS, 1, HEAD_DIM, tm)

    def resident(shape):
        return pl.BlockSpec(shape, lambda i: (0,) * len(shape), pipeline_mode=pl.Buffered(1))

    return pl.pallas_call(
        functools.partial(_out_ffn_body, final_norm=final_norm),
        grid=(b * ns,),
        in_specs=[pl.BlockSpec((tm, D_MODEL), row),
                  pl.BlockSpec(att_blk, tr),
                  pl.BlockSpec((tm, GROUP_WIDTH), row),
                  pl.BlockSpec(att_blk, tr),
                  pl.BlockSpec((tm, GROUP_WIDTH), row),
                  resident((4, GROUP_WIDTH, D_MODEL)),
                  resident((1, D_MODEL)),
                  resident((D_MODEL, FFN_HIDDEN)),
                  resident((D_MODEL, FFN_HIDDEN)),
                  resident((FFN_HIDDEN, D_MODEL)),
                  resident((1, D_MODEL))],
        out_specs=pl.BlockSpec((tm, D_MODEL), row),
        out_shape=jax.ShapeDtypeStruct((b * s, D_MODEL), F32),
        compiler_params=_cparams(("parallel",)),
        name="out_ffn",
    )(x2, yaT, yb, ycT, yd, wo, n2, wg, wu, wd, fn)


def _prep_w_in(w):
    z = lambda n: jnp.zeros((w.shape[0], n), w.dtype)
    small = jnp.concatenate([w[:, 768:772], w[:, 2996:3000], z(8), w[:, 1540:1556], z(32),
                             w[:, 1940:1972], z(32)], axis=1)
    out = jnp.concatenate([w[:, 0:768], w[:, 772:1540], w[:, 1556:1940], w[:, 1972:2996], small], axis=1)
    return out.astype(BF16)


def _lane_row(vals, lane0):
    return jnp.zeros((1, LANES), F32).at[0, lane0:lane0 + vals.shape[0]].set(vals.astype(F32))


def _prep_mla_w(w_uq, w_ukv):
    wq = w_uq.reshape(MLA_Q_LORA, HEADS, HEAD_DIM + MLA_ROPE)
    wq = jnp.pad(wq, ((0, 0), (0, 0), (0, LANES - HEAD_DIM - MLA_ROPE))).reshape(MLA_Q_LORA, HEADS * LANES)
    wkv = w_ukv.reshape(MLA_KV_LORA, HEADS, 2 * HEAD_DIM)
    wk = jnp.pad(wkv[:, :, :HEAD_DIM], ((0, 0), (0, 0), (0, LANES - HEAD_DIM))).reshape(MLA_KV_LORA, HEADS * LANES)
    wv = wkv[:, :, HEAD_DIM:].reshape(MLA_KV_LORA, GROUP_WIDTH)
    return wq.astype(BF16), wk.astype(BF16), wv.astype(BF16)


def kernel(x, positions, norm1, w_in, fox_f_bias, gla_gate_w2, gla_gate_b, gla_out_norm, mla_q_norm,
           mla_w_uq, mla_kv_norm, mla_w_ukv, ssm_conv_w, ssm_conv_b, ssm_dt_bias, ssm_A_log, ssm_D,
           ssm_norm, w_out, norm2, w_gate, w_up, w_down, final_norm):
    b, s, d = x.shape
    depth = w_in.shape[0]
    assert d == D_MODEL and s % TILE_ROWS == 0 and ATT_TQ == TILE_ROWS and ATT_TQ % ATT_TK == 0
    t = b * s
    x2 = x.reshape(t, d)

    half = MLA_ROPE // 2
    inv = ROPE_THETA ** (-jnp.arange(half, dtype=F32) / half)
    invf = jnp.tile(jnp.concatenate([inv, inv]), LANES // MLA_ROPE)[None, :]
    cos_t, sin_t = _rope_tables(positions.reshape(t, 1), invf)

    tri_att = jnp.asarray(np.tril(np.ones((CUMSUM_ROWS, CUMSUM_ROWS), np.float32)), BF16)

    for l in range(depth):
        wq, wk, wv = _prep_mla_w(mla_w_uq[l], mla_w_ukv[l])
        gla, ssm, small, fqT, fkA, fvT, mqT, mkA, mvT = _inproj(
            x2, norm1[l][None, :], _prep_w_in(w_in[l]), _lane_row(fox_f_bias[l], SMALL_FOX_F), tri_att,
            cos_t, sin_t, mla_q_norm[l][None, :], mla_kv_norm[l][None, :], wq, wk, wv, b, s)
        ya = _flash(fqT, fkA, fvT)
        yc = _flash(mqT, mkA, mvT)

        w2p = jnp.zeros((LANES, HEADS * GLA_DK), F32).at[SMALL_GATE:SMALL_GATE + GLA_GATE_RANK].set(
            gla_gate_w2[l]).astype(BF16)
        yb = _gla(gla, small, w2p, gla_gate_b[l][None, :].astype(F32),
                  jnp.tile(gla_out_norm[l], HEADS)[None, :].astype(F32), b, s)

        yd = _ssd(ssm, small, ssm_conv_w[l], ssm_conv_b[l][None, :],
                  _lane_row(ssm_dt_bias[l], SMALL_DT), _lane_row(ssm_A_log[l], SMALL_DT),
                  jnp.repeat(ssm_D[l], HEAD_DIM)[None, :].astype(F32), ssm_norm[l][None, :], b, s)

        x2 = _out_ffn(x2, ya, yb, yc, yd, w_out[l].reshape(4, GROUP_WIDTH, D_MODEL).astype(BF16),
                      norm2[l][None, :], w_gate[l].astype(BF16), w_up[l].astype(BF16),
                      w_down[l].astype(BF16), final_norm[None, :], b, s,
                      final_norm=(l == depth - 1))
    return x2.reshape(b, s, d)
```

```python
import functools
import math

import numpy as np
import jax
import jax.numpy as jnp
from jax import lax
from jax.experimental import pallas as pl
from jax.experimental.pallas import tpu as pltpu

F32 = jnp.float32
BF16 = jnp.bfloat16

D_MODEL = 1024
GROUP_WIDTH = 256
HEADS = 4
HEAD_DIM = 64
GLA_DK = 32
GLA_GATE_RANK = 16
GLA_GATE_TAU = 16.0
GLA_CHUNK = 64
GLA_SUB = 8
MLA_ROPE = 32
MLA_Q_LORA = 256
MLA_KV_LORA = 128
ROPE_THETA = 10000.0
SSM_STATE = 128
SSM_CONV = 4
SSM_CHUNK = 128
SSM_CONV_DIM = 768
FFN_HIDDEN = 2816
EPS = 1e-6
NEG = -1e30
LOG2E = math.log2(math.e)
SUM_ROWS = 16

LANES = 128
QK_PAD = 128
VMEM_LIMIT = 56 * 1024 * 1024

SEG_FOXQ = (0, 256)
SEG_FOXKV = (256, 768)
SEG_GLA = (768, 1536)
SEG_MLA = (1536, 1920)
SEG_SSM = (1920, 2944)
SEG_SMALL = (2944, 3072)
IN_PAD = 3072
SMALL_FOX_F = 0
SMALL_DT = 4
SMALL_GATE = 16
SMALL_KROPE = 64

TILE_ROWS = 512
ATT_TQ = 512
ATT_TK = 256
FLASH_UNROLL = 4
SCAN_ROWS = 512
GLA_ROWS = 1024
CUMSUM_ROWS = 256


def _cparams(sem):
    return pltpu.CompilerParams(dimension_semantics=sem, vmem_limit_bytes=VMEM_LIMIT)


def _rms(x, g):
    ms = jnp.mean(x * x, axis=-1, keepdims=True)
    return x * lax.rsqrt(ms + EPS) * g


def _log_sigmoid(x):
    return jnp.minimum(x, 0.0) - jnp.log1p(jnp.exp(-jnp.abs(x)))


def _softplus(x):
    return jnp.maximum(x, 0.0) + jnp.log1p(jnp.exp(-jnp.abs(x)))


def _silu(x):
    return x / (1.0 + jnp.exp(-x))


def _split3(x):
    hi = x.astype(BF16)
    r = x - hi.astype(F32)
    mid = r.astype(BF16)
    lo = (r - mid.astype(F32)).astype(BF16)
    return hi, mid, lo


def _dot(a, b):
    return jnp.dot(a, b, preferred_element_type=F32)


def _dot_nt(a, b):
    return lax.dot_general(a, b, (((1,), (1,)), ((), ())), preferred_element_type=F32)


def _dot_tn(a, b):
    return lax.dot_general(a, b, (((0,), (0,)), ((), ())), preferred_element_type=F32)


def _exact_lhs_dot(m01, x, wide=True):
    hi, mid, lo = _split3(x)
    if not wide:
        return _dot(m01, hi) + _dot(m01, mid) + _dot(m01, lo)
    w = x.shape[1]
    y = _dot(m01, jnp.concatenate([hi, mid, lo], axis=1))
    return y[:, 0:w] + y[:, w:2 * w] + y[:, 2 * w:3 * w]


def _inproj_body(x_ref, g_ref, w_ref, fb_ref, tri_ref, cos_ref, sin_ref, qn_ref, kvn_ref, wq_ref, wk_ref,
                 wv_ref, gla_ref, ssm_ref, small_ref, fqT_ref, fkA_ref, fvT_ref, mqT_ref, mkA_ref, mvT_ref,
                 carry_ref, *, tiles_per_seq):
    @pl.when(pl.program_id(0) % tiles_per_seq == 0)
    def _():
        carry_ref[...] = jnp.zeros_like(carry_ref)

    h = _rms(x_ref[...], g_ref[...]).astype(BF16)

    def seg(s):
        return _dot(h, w_ref[:, s[0]:s[1]])

    small = seg(SEG_SMALL)
    small_ref[...] = small
    foxq, foxkv, mla = seg(SEG_FOXQ), seg(SEG_FOXKV).astype(BF16), seg(SEG_MLA)
    gla_ref[...] = seg(SEG_GLA)
    _fox_prep_body(foxq, foxkv, small, fb_ref, tri_ref, fqT_ref, fkA_ref, fvT_ref, carry_ref)
    ssm_ref[...] = seg(SEG_SSM)
    _mla_prep_body(mla, small, cos_ref, sin_ref, qn_ref, kvn_ref, wq_ref, wk_ref, wv_ref,
                   mqT_ref, mkA_ref, mvT_ref)


def _inproj(x2, g, w, fbias, tri, cos_t, sin_t, qn, kvn, wq, wk, wv, b, s):
    t = x2.shape[0]
    tm = TILE_ROWS
    ns = s // tm
    row = lambda i: (i, 0)
    full = lambda i: (0, 0)
    widths = [sg[1] - sg[0] for sg in (SEG_GLA, SEG_SSM, SEG_SMALL)]
    att_specs = [pl.BlockSpec((1, HEADS, 1, QK_PAD, tm), lambda i: (i // ns, 0, i % ns, 0, 0)),
                 pl.BlockSpec((1, HEADS, tm, QK_PAD), lambda i: (i // ns, 0, i % ns, 0)),
                 pl.BlockSpec((1, 1, tm // ATT_TK, GROUP_WIDTH, ATT_TK), lambda i: (i // ns, i % ns, 0, 0, 0))]
    att_shapes = [jax.ShapeDtypeStruct((b, HEADS, ns, QK_PAD, tm), BF16),
                  jax.ShapeDtypeStruct((b, HEADS, s, QK_PAD), BF16),
                  jax.ShapeDtypeStruct((b, ns, tm // ATT_TK, GROUP_WIDTH, ATT_TK), BF16)]
    return pl.pallas_call(
        functools.partial(_inproj_body, tiles_per_seq=ns),
        grid=(t // tm,),
        in_specs=[pl.BlockSpec((tm, D_MODEL), row),
                  pl.BlockSpec((1, D_MODEL), full),
                  pl.BlockSpec((D_MODEL, IN_PAD), full),
                  pl.BlockSpec((1, LANES), full),
                  pl.BlockSpec(tri.shape, full),
                  pl.BlockSpec((tm, LANES), row),
                  pl.BlockSpec((tm, LANES), row),
                  pl.BlockSpec((1, MLA_Q_LORA), full),
                  pl.BlockSpec((1, MLA_KV_LORA), full),
                  pl.BlockSpec((MLA_Q_LORA, HEADS * LANES), full),
                  pl.BlockSpec((MLA_KV_LORA, HEADS * LANES), full),
                  pl.BlockSpec((MLA_KV_LORA, GROUP_WIDTH), full)],
        out_specs=[pl.BlockSpec((tm, w_), row) for w_ in widths] + att_specs + att_specs,
        out_shape=[jax.ShapeDtypeStruct((t, w_), F32) for w_ in widths] + att_shapes + att_shapes,
        scratch_shapes=[pltpu.VMEM((1, LANES), F32)],
        compiler_params=_cparams(("arbitrary",)),
        name="inproj",
    )(x2, g, w, fbias, tri, cos_t, sin_t, qn, kvn, wq, wk, wv)


def _rope_table_body(pos_ref, invf_ref, cos_ref, sin_ref):
    tm = pos_ref.shape[0]
    groups = LANES // MLA_ROPE
    q = tm // groups
    lane = lax.broadcasted_iota(jnp.int32, (q, LANES), 1)
    pos = pos_ref[...].astype(F32)
    packed = jnp.zeros((q, LANES), F32)
    for m in range(groups):
        packed = jnp.where(lane // MLA_ROPE == m, jnp.broadcast_to(pos[m * q:(m + 1) * q], (q, LANES)), packed)
    ang = packed * invf_ref[...]
    cos_p, sin_p = jnp.cos(ang), jnp.sin(ang)
    inside = (lane >= 64) & (lane < 64 + MLA_ROPE)
    for m in range(groups):
        shift = (64 - MLA_ROPE * m) % LANES
        move = (lambda t: t) if shift == 0 else (lambda t: pltpu.roll(t, shift, 1))
        cos_ref[m * q:(m + 1) * q, :] = jnp.where(inside, move(cos_p), 1.0)
        sin_ref[m * q:(m + 1) * q, :] = jnp.where(inside, move(sin_p), 0.0)


def _rope_tables(pos_col, invf):
    t = pos_col.shape[0]
    tm = TILE_ROWS
    return pl.pallas_call(
        _rope_table_body,
        grid=(t // tm,),
        in_specs=[pl.BlockSpec((tm, 1), lambda i: (i, 0)),
                  pl.BlockSpec((1, LANES), lambda i: (0, 0))],
        out_specs=[pl.BlockSpec((tm, LANES), lambda i: (i, 0))] * 2,
        out_shape=[jax.ShapeDtypeStruct((t, LANES), F32)] * 2,
        compiler_params=_cparams(("parallel",)),
        name="rope_tables",
    )(pos_col, invf)


def _transpose_to_bf16(x):
    eye = (lax.broadcasted_iota(jnp.int32, (LANES, LANES), 0)
           == lax.broadcasted_iota(jnp.int32, (LANES, LANES), 1)).astype(BF16)
    return _dot_nt(eye, x.astype(BF16)).astype(BF16)


def _store_vT(vT_ref, v_all):
    for blk in range(v_all.shape[0] // ATT_TK):
        for pair in range(2):
            vp = v_all[blk * ATT_TK:(blk + 1) * ATT_TK, pair * LANES:(pair + 1) * LANES]
            vT_ref[0, 0, blk, pair * LANES:(pair + 1) * LANES, :] = _transpose_to_bf16(vp)


def _fox_prep_body(foxq_ref, foxkv_ref, small_ref, fb_ref, tri_ref, qT_ref, kA_ref, vT_ref, carry_ref):
    ts = foxq_ref.shape[0]

    logf = _log_sigmoid(small_ref[...] + fb_ref[...])
    blk = tri_ref.shape[0]
    pieces, run = [], carry_ref[...]
    for i0 in range(0, ts, blk):
        part = _exact_lhs_dot(tri_ref[...], logf[i0:i0 + blk]) + run
        run = part[blk - 1:blk, :]
        pieces.append(part)
    fcum = jnp.concatenate(pieces, axis=0)
    carry_ref[...] = run
    f_hi, f_mid, f_lo = [p.astype(F32) for p in _split3(fcum * LOG2E)]

    lane = lax.broadcasted_iota(jnp.int32, (ts, LANES), 1)
    low = lane < HEAD_DIM
    packed = jnp.where(lane < HEADS, f_hi,
             jnp.where(lane < 2 * HEADS, pltpu.roll(f_mid, HEADS, 1), pltpu.roll(f_lo, 2 * HEADS, 1)))
    f_q = pltpu.roll(packed, HEAD_DIM, 1)
    f_k = -pltpu.roll(packed, HEAD_DIM + 3 * HEADS, 1)
    _store_vT(vT_ref, foxkv_ref[:, 256:512].astype(F32))
    for pair in range(2):
        qp = foxq_ref[:, pair * LANES:(pair + 1) * LANES] * (LOG2E * HEAD_DIM ** -0.5)
        kp = foxkv_ref[:, pair * LANES:(pair + 1) * LANES].astype(F32)
        for e in range(2):
            h = 2 * pair + e
            qh = qp if e == 0 else pltpu.roll(qp, HEAD_DIM, 1)
            kh = kp if e == 0 else pltpu.roll(kp, HEAD_DIM, 1)
            mine = (lane - HEAD_DIM) % HEADS == h
            first = mine & (lane >= HEAD_DIM) & (lane < HEAD_DIM + 3 * HEADS)
            second = mine & (lane >= HEAD_DIM + 3 * HEADS) & (lane < HEAD_DIM + 6 * HEADS)
            qa = jnp.where(low, qh, jnp.where(first, f_q, jnp.where(second, 1.0, 0.0)))
            ka = jnp.where(low, kh, jnp.where(first, 1.0, jnp.where(second, f_k, 0.0)))
            qT_ref[0, h, 0] = _transpose_to_bf16(qa)
            kA_ref[0, h] = ka.astype(BF16)


def _mla_prep_body(mla_ref, small_ref, cos_ref, sin_ref, qn_ref, kvn_ref, wq_ref, wk_ref, wv_ref,
                   qT_ref, kA_ref, vT_ref):
    ts = mla_ref.shape[0]
    lane = lax.broadcasted_iota(jnp.int32, (ts, LANES), 1)
    cosv = cos_ref[...]
    sinv = sin_ref[...]
    half = MLA_ROPE // 2
    sin_a = jnp.where((lane >= 64) & (lane < 64 + half), -sinv, 0.0)
    sin_b = jnp.where((lane >= 64 + half) & (lane < 64 + MLA_ROPE), sinv, 0.0)

    def rope(x):
        return (x * cosv + pltpu.roll(x, LANES - half, 1) * sin_a + pltpu.roll(x, half, 1) * sin_b)

    cq = _rms(mla_ref[:, 0:MLA_Q_LORA], qn_ref[...]).astype(BF16)
    ckv = _rms(mla_ref[:, MLA_Q_LORA:MLA_Q_LORA + MLA_KV_LORA], kvn_ref[...]).astype(BF16)
    kr = rope(jnp.where((lane >= 64) & (lane < 64 + MLA_ROPE), small_ref[...], 0.0))
    q_all = _dot(cq, wq_ref[...]) * (LOG2E * (HEAD_DIM + MLA_ROPE) ** -0.5)
    k_all = _dot(ckv, wk_ref[...])
    for h in range(HEADS):
        qh = rope(q_all[:, h * LANES:(h + 1) * LANES])
        qT_ref[0, h, 0] = _transpose_to_bf16(qh)
        kA_ref[0, h] = (k_all[:, h * LANES:(h + 1) * LANES] + kr).astype(BF16)
    _store_vT(vT_ref, _dot(ckv, wv_ref[...]))


def _flash_body(qT_ref, k_ref, vT_ref, o_ref, m_ref, acc_ref, s_ref, p_ref, a_ref, mb_ref, *, tq, tk):
    nq = qT_ref.shape[2]
    nb = tq // tk
    assert nb % 2 == 0
    ones = jnp.ones((SUM_ROWS, tk), BF16)

    def score_block(qi, j, slot, lo=0, with_max=True):
        start = pl.multiple_of(j * tk, tk)
        s = _dot(k_ref[0, 0, pl.ds(start, tk), :], qT_ref[0, 0, qi, :, lo:])
        s_ref[slot, :, lo:] = s
        if with_max:
            mb_ref[slot] = jnp.max(s, axis=0, keepdims=True)

    score_block(0, 0, 0)

    def q_tile(i, carry):
        m_ref[...] = jnp.full_like(m_ref, NEG)
        acc_ref[...] = jnp.zeros_like(acc_ref)
        p_ref[1] = jnp.zeros_like(p_ref[1])
        a_ref[1] = jnp.ones_like(a_ref[1])
        n_full = nb * i

        def scores(j, slot, lo=0, with_max=True):
            score_block(i, j, slot, lo, with_max)

        def softmax(slot, lo=0, diag=False):
            s = s_ref[slot, :, lo:]
            if diag:
                keep = (lax.broadcasted_iota(jnp.int32, (tk, tk), 0)
                        <= lax.broadcasted_iota(jnp.int32, (tk, tk), 1))
                left = jnp.where(keep, s[:, :tk], NEG)
                s = left if s.shape[1] == tk else jnp.concatenate([left, s[:, tk:]], axis=1)
                m_blk = jnp.max(s, axis=0, keepdims=True)
            else:
                m_blk = mb_ref[slot]
            m_prev = m_ref[:, lo:]
            m_new = jnp.maximum(m_prev, m_blk)
            a_ref[slot, :, lo:] = jnp.exp2(m_prev - m_new)
            p_ref[slot, :, lo:] = jnp.exp2(s - m_new).astype(BF16)
            m_ref[:, lo:] = m_new

        def pv(j, slot, lo=0):
            v_ext = jnp.concatenate([vT_ref[0, j, 0], ones], axis=0)
            acc_ref[:, lo:] = (a_ref[slot, :, lo:] * acc_ref[:, lo:]
                               + _dot(v_ext, p_ref[slot, :, lo:]))

        def run_blocks(j, count):
            for u in range(count):
                scores(j + u + 1, (u + 1) % 2)
                softmax(u % 2)
                pv(jnp.maximum(j + u - 1, 0), (u + 1) % 2)

        per_trip = nb * FLASH_UNROLL
        n_trips = i // FLASH_UNROLL

        def body(t, c2):
            run_blocks(per_trip * t, per_trip)
            return c2

        lax.fori_loop(0, n_trips, body, 0)
        done = per_trip * n_trips
        part = FLASH_UNROLL // 2
        while part >= 1:
            take = (i % (2 * part)) >= part

            @pl.when(take)
            def _(done=done, part=part):
                run_blocks(done, nb * part)

            done = done + jnp.where(take, nb * part, 0)
            part //= 2
        for u in range(nb):
            if u + 1 < nb:
                scores(n_full + u + 1, (u + 1) % 2, lo=(u + 1) * tk, with_max=False)
            softmax(u % 2, lo=u * tk, diag=True)
            pv(jnp.maximum(n_full + u - 1, 0), (u + 1) % 2, lo=max(u - 1, 0) * tk)
        score_block(jnp.minimum(i + 1, nq - 1), 0, 0)
        pv(n_full + nb - 1, (nb - 1) % 2, lo=(nb - 1) * tk)
        o_ref[0, 0, i] = (acc_ref[0:HEAD_DIM, :] / acc_ref[HEAD_DIM:HEAD_DIM + 1, :]).astype(o_ref.dtype)
        return carry

    lax.fori_loop(0, nq, q_tile, 0)


def _flash(qT, kA, vT):
    b, _, nq, _, tq = qT.shape
    tk = ATT_TK
    s = nq * tq
    nk = s // tk
    vT5 = vT.reshape(b, nk, HEADS, HEAD_DIM, tk)
    return pl.pallas_call(
        functools.partial(_flash_body, tq=tq, tk=tk),
        grid=(b, HEADS),
        in_specs=[pl.BlockSpec((1, 1, nq, QK_PAD, tq), lambda bi, h: (bi, h, 0, 0, 0)),
                  pl.BlockSpec((1, 1, s, QK_PAD), lambda bi, h: (bi, h, 0, 0)),
                  pl.BlockSpec((1, nk, 1, HEAD_DIM, tk), lambda bi, h: (bi, 0, h, 0, 0))],
        out_specs=pl.BlockSpec((1, 1, nq, HEAD_DIM, tq), lambda bi, h: (bi, h, 0, 0, 0)),
        out_shape=jax.ShapeDtypeStruct((b, HEADS, nq, HEAD_DIM, tq), BF16),
        scratch_shapes=[pltpu.VMEM((1, tq), F32),
                        pltpu.VMEM((HEAD_DIM + SUM_ROWS, tq), F32),
                        pltpu.VMEM((2, tk, tq), F32),
                        pltpu.VMEM((2, tk, tq), BF16),
                        pltpu.VMEM((2, 1, tq), F32),
                        pltpu.VMEM((2, 1, tq), F32)],
        compiler_params=_cparams(("parallel", "parallel")),
        name="flash",
    )(qT, kA, vT5)


def _gla_consts(ts):
    idx = np.arange(ts)
    same = (idx[:, None] // GLA_CHUNK) == (idx[None, :] // GLA_CHUNK)
    mbig = (same & (idx[None, :] <= idx[:, None])).astype(np.float32)
    hd = np.arange(HEADS * GLA_DK) // GLA_DK
    hv = np.arange(GROUP_WIDTH) // HEAD_DIM
    ind_dk_dv = (hd[:, None] == hv[None, :]).astype(np.float32)
    ind_mean = (hv[:, None] == hv[None, :]).astype(np.float32) / HEAD_DIM
    return (jnp.asarray(mbig, BF16), jnp.asarray(ind_dk_dv, BF16), jnp.asarray(ind_mean, BF16))


def _pick_heads(stacked, rows):
    lane_h = lax.broadcasted_iota(jnp.int32, (rows, GROUP_WIDTH), 1) // HEAD_DIM
    out = jnp.zeros((rows, GROUP_WIDTH), F32)
    for h in range(HEADS):
        out = jnp.where(lane_h == h, stacked[h * rows:(h + 1) * rows, :], out)
    return out


def _gla_body(gla_ref, small_ref, w2_ref, b2_ref, on_ref, mbig_ref, ind_ref, indm_ref, o_ref, st_ref,
              gpad_ref, kpad_ref, vpad_ref):
    ts = gla_ref.shape[0]
    L = GLA_CHUNK
    nsub = L // GLA_SUB
    PAD = GLA_SUB

    @pl.when(pl.program_id(1) == 0)
    def _():
        st_ref[...] = jnp.zeros_like(st_ref)

    q = gla_ref[:, 0:128] * (GLA_DK ** -0.5)
    k = gla_ref[:, 128:256]
    v = gla_ref[:, 256:512]
    r = gla_ref[:, 512:768]
    x = _dot(small_ref[...].astype(BF16), w2_ref[...]) + b2_ref[...]
    g = _log_sigmoid(x) * (1.0 / GLA_GATE_TAU)
    cb = mbig_ref.shape[0]
    G = jnp.concatenate([_exact_lhs_dot(mbig_ref[...], g[i0:i0 + cb]) for i0 in range(0, ts, cb)], axis=0)
    r_rows, gl_rows = [], []
    for c in range(ts // L):
        for I in range(nsub):
            at = c * L + I * GLA_SUB
            r_rows.append(jnp.zeros((GLA_SUB, 128), F32) if I == 0
                          else jnp.broadcast_to(G[at - 1:at, :], (GLA_SUB, 128)))
        gl_rows.append(jnp.broadcast_to(G[(c + 1) * L - 1:(c + 1) * L, :], (L, 128)))
    R = jnp.concatenate(r_rows, axis=0)
    GL = jnp.concatenate(gl_rows, axis=0)
    q_inter = q * jnp.exp(G)
    q_norm = q * jnp.exp(G - R)
    k_dec = k * jnp.exp(GL - G)
    for ref, val in ((gpad_ref, G), (kpad_ref, k), (vpad_ref, v)):
        ref[0:PAD, :] = jnp.zeros((PAD, val.shape[1]), F32)
        ref[PAD:PAD + ts, :] = val

    lane_h = lax.broadcasted_iota(jnp.int32, (L, 128), 1) // GLA_DK
    row = lax.broadcasted_iota(jnp.int32, (L, 128), 0)
    sub_pos = row % GLA_SUB
    sub_idx = row // GLA_SUB
    row_blk = lax.broadcasted_iota(jnp.int32, (L, L * 2), 0) // GLA_SUB
    zeros_v = jnp.zeros((L, GROUP_WIDTH), BF16)

    def stack_heads(a):
        return jnp.concatenate([jnp.where(lane_h == h, a, 0.0) for h in range(HEADS)], axis=0)

    outs = []
    for c in range(ts // L):
        sl = slice(c * L, (c + 1) * L)
        Gc, kc, vc, qc = G[sl], k[sl], v[sl], q[sl]
        vc16 = vc.astype(BF16)

        r_next = jnp.concatenate([R[c * L + GLA_SUB:(c + 1) * L], R[(c + 1) * L - GLA_SUB:(c + 1) * L]],
                                 axis=0)
        base = kc * jnp.exp(jnp.where(sub_idx < nsub - 1, r_next - Gc, NEG))
        kts = []
        kt = jnp.zeros_like(base)
        for I in range(1, nsub):
            if I > 1:
                r_i = R[c * L + I * GLA_SUB:c * L + I * GLA_SUB + 1, :]
                r_p = R[c * L + (I - 1) * GLA_SUB:c * L + (I - 1) * GLA_SUB + 1, :]
                kt = kt * jnp.exp(r_i - r_p)
            kt = jnp.where(sub_idx == I - 1, base, kt)
            kts.append(kt)
        kts.append(jnp.zeros_like(base))
        kstack = jnp.concatenate(kts, axis=0).astype(BF16)
        sc = _dot_nt(stack_heads(q_norm[sl]).astype(BF16), kstack)
        a_rows = []
        for h in range(HEADS):
            sh = sc[h * L:(h + 1) * L]
            a_h = jnp.zeros((L, 2 * L), F32)
            for tile in range(nsub // 2):
                mine = (row_blk == 2 * tile + 1) | (row_blk == 2 * tile + 2)
                a_h = jnp.where(mine, sh[:, tile * 2 * L:(tile + 1) * 2 * L], a_h)
            a_rows.append(jnp.where(row_blk % 2 == 0, pltpu.roll(a_h, L, 1), a_h))
        a_st = jnp.concatenate(a_rows, axis=0).astype(BF16)
        v_pad = jnp.concatenate([vc16, zeros_v], axis=0)
        st = st_ref[...]
        big = _dot(a_st, v_pad) + _dot_nt(stack_heads(q_inter[sl]).astype(BF16), st.astype(BF16))
        o_c = _pick_heads(big, L)

        ps = []
        for d in range(GLA_SUB):
            if d == 0:
                p = qc * kc
            else:
                lo = PAD + c * L - d
                arg = jnp.where(sub_pos >= d, Gc - gpad_ref[lo:lo + L, :], NEG)
                p = qc * kpad_ref[lo:lo + L, :] * jnp.exp(arg)
            ps.append(p)
        abc = _dot(jnp.concatenate(ps, axis=0).astype(BF16), ind_ref[...])
        for d in range(GLA_SUB):
            lo = PAD + c * L - d
            o_c = o_c + abc[d * L:(d + 1) * L] * vpad_ref[lo:lo + L, :]
        outs.append(o_c)

        st_ref[...] = (jnp.exp(GL[c * L:c * L + 1, :]) * st
                       + _dot_tn(vc16, k_dec[sl].astype(BF16)))

    o = jnp.concatenate(outs, axis=0)
    o2 = o * o
    hi = o2.astype(BF16)
    lo = (o2 - hi.astype(F32)).astype(BF16)
    ms = _dot(hi, indm_ref[...]) + _dot(lo, indm_ref[...])
    o_ref[...] = (o * lax.rsqrt(ms + EPS) * on_ref[...] * _silu(r)).astype(o_ref.dtype)


def _gla(gla, small, w2p, b2, onorm, b, s):
    ts = GLA_ROWS
    ns = s // ts
    mbig, ind, indm = _gla_consts(CUMSUM_ROWS)
    row = lambda bi, i: (bi * ns + i, 0)
    full = lambda bi, i: (0, 0)
    return pl.pallas_call(
        _gla_body,
        grid=(b, ns),
        in_specs=[pl.BlockSpec((ts, 768), row),
                  pl.BlockSpec((ts, LANES), row),
                  pl.BlockSpec((LANES, 128), full),
                  pl.BlockSpec((1, 128), full),
                  pl.BlockSpec((1, GROUP_WIDTH), full),
                  pl.BlockSpec(mbig.shape, full),
                  pl.BlockSpec((128, GROUP_WIDTH), full),
                  pl.BlockSpec((GROUP_WIDTH, GROUP_WIDTH), full)],
        out_specs=pl.BlockSpec((ts, GROUP_WIDTH), row),
        out_shape=jax.ShapeDtypeStruct((b * s, GROUP_WIDTH), BF16),
        scratch_shapes=[pltpu.VMEM((GROUP_WIDTH, 128), F32),
                        pltpu.VMEM((GLA_SUB + ts, 128), F32),
                        pltpu.VMEM((GLA_SUB + ts, 128), F32),
                        pltpu.VMEM((GLA_SUB + ts, GROUP_WIDTH), F32)],
        compiler_params=_cparams(("parallel", "arbitrary")),
        name="gla",
    )(gla, small, w2p, b2, onorm, mbig, ind, indm)


def _head_cols(src, lane0, rows, width, nheads):
    lane_h = lax.broadcasted_iota(jnp.int32, (rows, nheads * width), 1) // width
    out = jnp.broadcast_to(src[:, lane0:lane0 + 1], (rows, nheads * width))
    for h in range(1, nheads):
        out = jnp.where(lane_h == h, jnp.broadcast_to(src[:, lane0 + h:lane0 + h + 1],
                                                      (rows, nheads * width)), out)
    return out


def _ssd_body(ssm_ref, small_ref, cw_ref, cb_ref, dtb_ref, alog_ref, dskip_ref, nw_ref, tri_ref,
              o_ref, prev_ref, st_ref):
    ts = ssm_ref.shape[0]
    L = SSM_CHUNK

    @pl.when(pl.program_id(1) == 0)
    def _():
        prev_ref[...] = jnp.zeros_like(prev_ref)
        st_ref[...] = jnp.zeros_like(st_ref)

    z = ssm_ref[:, 0:GROUP_WIDTH]
    xbc = ssm_ref[:, GROUP_WIDTH:GROUP_WIDTH + SSM_CONV_DIM]
    prev8 = prev_ref[...]
    row8 = lax.broadcasted_iota(jnp.int32, (8, SSM_CONV_DIM), 0)
    conv = cb_ref[...] + cw_ref[SSM_CONV - 1:SSM_CONV, :] * xbc
    for sft in range(1, SSM_CONV):
        rolled = pltpu.roll(xbc, sft, 0)
        top = jnp.where(row8 < sft, pltpu.roll(prev8, sft, 0), rolled[0:8])
        shifted = jnp.concatenate([top, rolled[8:]], axis=0)
        conv = conv + cw_ref[SSM_CONV - 1 - sft:SSM_CONV - sft, :] * shifted
    prev_ref[...] = xbc[ts - 8:ts]
    xc = _silu(conv)
    xs = xc[:, 0:GROUP_WIDTH]
    bm = xc[:, GROUP_WIDTH:GROUP_WIDTH + 2 * SSM_STATE]
    cm = xc[:, GROUP_WIDTH + 2 * SSM_STATE:]

    dt = _softplus(small_ref[...] + dtb_ref[...])
    a = -jnp.exp(alog_ref[...]) * dt
    xdt = xs * _head_cols(dt, SMALL_DT, ts, HEAD_DIM, HEADS)

    ii = lax.broadcasted_iota(jnp.int32, (L, L), 0)
    jj = lax.broadcasted_iota(jnp.int32, (L, L), 1)
    lane_lo = jj < HEAD_DIM
    row_lo = ii < HEAD_DIM
    tri = tri_ref[...]
    ys = []
    for c in range(ts // L):
        sl = slice(c * L, (c + 1) * L)
        cs = _exact_lhs_dot(tri, a[sl], wide=False)
        cs_t = cs.T
        cs_last = cs[L - 1:L, :]
        y_groups = []
        for grp in range(2):
            h0 = 2 * grp
            b_g = bm[sl, grp * SSM_STATE:(grp + 1) * SSM_STATE].astype(BF16)
            c_g = cm[sl, grp * SSM_STATE:(grp + 1) * SSM_STATE].astype(BF16)
            x_g = xdt[sl, grp * LANES:(grp + 1) * LANES]
            cb = _dot_nt(c_g, b_g)
            ws = []
            for h in (h0, h0 + 1):
                colh = cs[:, SMALL_DT + h:SMALL_DT + h + 1]
                rowh = cs_t[SMALL_DT + h:SMALL_DT + h + 1, :]
                ws.append(cb * jnp.exp(jnp.where(jj <= ii, colh - rowh, NEG)))
            yd = _dot(jnp.concatenate(ws, axis=0).astype(BF16), x_g.astype(BF16))
            y_diag = jnp.where(lane_lo, yd[0:L], yd[L:2 * L])
            cs_g = _head_cols(cs, SMALL_DT + h0, L, HEAD_DIM, 2)
            st_g = st_ref[grp * LANES:(grp + 1) * LANES, :]
            y_off = _dot_nt(c_g, st_g.astype(BF16)) * jnp.exp(cs_g)
            y_groups.append(y_diag + y_off)
            last_g = _head_cols(cs_last, SMALL_DT + h0, 1, HEAD_DIM, 2)
            x_dec = x_g * jnp.exp(last_g - cs_g)
            new = _dot_tn(x_dec.astype(BF16), b_g)
            e0 = jnp.exp(cs_last[:, SMALL_DT + h0:SMALL_DT + h0 + 1])
            e1 = jnp.exp(cs_last[:, SMALL_DT + h0 + 1:SMALL_DT + h0 + 2])
            st_ref[grp * LANES:(grp + 1) * LANES, :] = jnp.where(row_lo, e0, e1) * st_g + new
        ys.append(jnp.concatenate(y_groups, axis=1))
    y = jnp.concatenate(ys, axis=0)
    y = (y + dskip_ref[...] * xs) * _silu(z)
    halves = []
    for grp in range(2):
        yg = y[:, grp * LANES:(grp + 1) * LANES]
        halves.append(_rms(yg, nw_ref[:, grp * LANES:(grp + 1) * LANES]))
    o_ref[...] = jnp.concatenate(halves, axis=1).astype(o_ref.dtype)


def _ssd(ssm, small, cw, cb, dtb, alog, dskip, nw, b, s):
    ts = SCAN_ROWS
    ns = s // ts
    tri = jnp.asarray(np.tril(np.ones((SSM_CHUNK, SSM_CHUNK), np.float32)), BF16)
    row = lambda bi, i: (bi * ns + i, 0)
    full = lambda bi, i: (0, 0)
    return pl.pallas_call(
        _ssd_body,
        grid=(b, ns),
        in_specs=[pl.BlockSpec((ts, 1024), row),
                  pl.BlockSpec((ts, LANES), row),
                  pl.BlockSpec((SSM_CONV, SSM_CONV_DIM), full),
                  pl.BlockSpec((1, SSM_CONV_DIM), full),
                  pl.BlockSpec((1, LANES), full),
                  pl.BlockSpec((1, LANES), full),
                  pl.BlockSpec((1, GROUP_WIDTH), full),
                  pl.BlockSpec((1, GROUP_WIDTH), full),
                  pl.BlockSpec((SSM_CHUNK, SSM_CHUNK), full)],
        out_specs=pl.BlockSpec((ts, GROUP_WIDTH), row),
        out_shape=jax.ShapeDtypeStruct((b * s, GROUP_WIDTH), BF16),
        scratch_shapes=[pltpu.VMEM((8, SSM_CONV_DIM), F32), pltpu.VMEM((GROUP_WIDTH, SSM_STATE), F32)],
        compiler_params=_cparams(("parallel", "arbitrary")),
        name="ssd",
    )(ssm, small, cw, cb, dtb, alog, dskip, nw, tri)


def _out_ffn_body(x_ref, yaT_ref, yb_ref, ycT_ref, yd_ref, wo_ref, n2_ref, wg_ref, wu_ref, wd_ref,
                  fn_ref, o_ref, *, final_norm):
    tm = x_ref.shape[0]
    ya_t = yaT_ref[0, :, 0].reshape(GROUP_WIDTH, tm)
    yc_t = ycT_ref[0, :, 0].reshape(GROUP_WIDTH, tm)
    x1 = (x_ref[...]
          + _dot_tn(ya_t, wo_ref[0]) + _dot(yb_ref[...], wo_ref[1])
          + _dot_tn(yc_t, wo_ref[2]) + _dot(yd_ref[...], wo_ref[3]))
    h2 = _rms(x1, n2_ref[...]).astype(BF16)
    hidden = _silu(_dot(h2, wg_ref[...])) * _dot(h2, wu_ref[...])
    out = x1 + _dot(hidden.astype(BF16), wd_ref[...])
    if final_norm:
        out = _rms(out, fn_ref[...])
    o_ref[...] = out


def _out_ffn(x2, yaT, yb, ycT, yd, wo, n2, wg, wu, wd, fn, b, s, final_norm):
    tm = TILE_ROWS
    ns = s // tm
    row = lambda i: (i, 0)
    tr = lambda i: (i // ns, 0, i % ns, 0, 0)
    att_blk = (1, HEADS, 1, HEAD_DIM, tm)

    def resident(shape):
        return pl.BlockSpec(shape, lambda i: (0,) * len(shape), pipeline_mode=pl.Buffered(1))

    return pl.pallas_call(
        functools.partial(_out_ffn_body, final_norm=final_norm),
        grid=(b * ns,),
        in_specs=[pl.BlockSpec((tm, D_MODEL), row),
                  pl.BlockSpec(att_blk, tr),
                  pl.BlockSpec((tm, GROUP_WIDTH), row),
                  pl.BlockSpec(att_blk, tr),
                  pl.BlockSpec((tm, GROUP_WIDTH), row),
                  resident((4, GROUP_WIDTH, D_MODEL)),
                  resident((1, D_MODEL)),
                  resident((D_MODEL, FFN_HIDDEN)),
                  resident((D_MODEL, FFN_HIDDEN)),
                  resident((FFN_HIDDEN, D_MODEL)),
                  resident((1, D_MODEL))],
        out_specs=pl.BlockSpec((tm, D_MODEL), row),
        out_shape=jax.ShapeDtypeStruct((b * s, D_MODEL), F32),
        compiler_params=_cparams(("parallel",)),
        name="out_ffn",
    )(x2, yaT, yb, ycT, yd, wo, n2, wg, wu, wd, fn)


def _prep_w_in(w):
    z = lambda n: jnp.zeros((w.shape[0], n), w.dtype)
    small = jnp.concatenate([w[:, 768:772], w[:, 2996:3000], z(8), w[:, 1540:1556], z(32),
                             w[:, 1940:1972], z(32)], axis=1)
    out = jnp.concatenate([w[:, 0:768], w[:, 772:1540], w[:, 1556:1940], w[:, 1972:2996], small], axis=1)
    return out.astype(BF16)


def _lane_row(vals, lane0):
    return jnp.zeros((1, LANES), F32).at[0, lane0:lane0 + vals.shape[0]].set(vals.astype(F32))


def _prep_mla_w(w_uq, w_ukv):
    wq = w_uq.reshape(MLA_Q_LORA, HEADS, HEAD_DIM + MLA_ROPE)
    wq = jnp.pad(wq, ((0, 0), (0, 0), (0, LANES - HEAD_DIM - MLA_ROPE))).reshape(MLA_Q_LORA, HEADS * LANES)
    wkv = w_ukv.reshape(MLA_KV_LORA, HEADS, 2 * HEAD_DIM)
    wk = jnp.pad(wkv[:, :, :HEAD_DIM], ((0, 0), (0, 0), (0, LANES - HEAD_DIM))).reshape(MLA_KV_LORA, HEADS * LANES)
    wv = wkv[:, :, HEAD_DIM:].reshape(MLA_KV_LORA, GROUP_WIDTH)
    return wq.astype(BF16), wk.astype(BF16), wv.astype(BF16)


def kernel(x, positions, norm1, w_in, fox_f_bias, gla_gate_w2, gla_gate_b, gla_out_norm, mla_q_norm,
           mla_w_uq, mla_kv_norm, mla_w_ukv, ssm_conv_w, ssm_conv_b, ssm_dt_bias, ssm_A_log, ssm_D,
           ssm_norm, w_out, norm2, w_gate, w_up, w_down, final_norm):
    b, s, d = x.shape
    depth = w_in.shape[0]
    assert d == D_MODEL and s % TILE_ROWS == 0 and ATT_TQ == TILE_ROWS and ATT_TQ % ATT_TK == 0
    t = b * s
    x2 = x.reshape(t, d)

    half = MLA_ROPE // 2
    inv = ROPE_THETA ** (-jnp.arange(half, dtype=F32) / half)
    invf = jnp.tile(jnp.concatenate([inv, inv]), LANES // MLA_ROPE)[None, :]
    cos_t, sin_t = _rope_tables(positions.reshape(t, 1), invf)

    tri_att = jnp.asarray(np.tril(np.ones((CUMSUM_ROWS, CUMSUM_ROWS), np.float32)), BF16)

    for l in range(depth):
        wq, wk, wv = _prep_mla_w(mla_w_uq[l], mla_w_ukv[l])
        gla, ssm, small, fqT, fkA, fvT, mqT, mkA, mvT = _inproj(
            x2, norm1[l][None, :], _prep_w_in(w_in[l]), _lane_row(fox_f_bias[l], SMALL_FOX_F), tri_att,
            cos_t, sin_t, mla_q_norm[l][None, :], mla_kv_norm[l][None, :], wq, wk, wv, b, s)
        ya = _flash(fqT, fkA, fvT)
        yc = _flash(mqT, mkA, mvT)

        w2p = jnp.zeros((LANES, HEADS * GLA_DK), F32).at[SMALL_GATE:SMALL_GATE + GLA_GATE_RANK].set(
            gla_gate_w2[l]).astype(BF16)
        yb = _gla(gla, small, w2p, gla_gate_b[l][None, :].astype(F32),
                  jnp.tile(gla_out_norm[l], HEADS)[None, :].astype(F32), b, s)

        yd = _ssd(ssm, small, ssm_conv_w[l], ssm_conv_b[l][None, :],
                  _lane_row(ssm_dt_bias[l], SMALL_DT), _lane_row(ssm_A_log[l], SMALL_DT),
                  jnp.repeat(ssm_D[l], HEAD_DIM)[None, :].astype(F32), ssm_norm[l][None, :], b, s)

        x2 = _out_ffn(x2, ya, yb, yc, yd, w_out[l].reshape(4, GROUP_WIDTH, D_MODEL).astype(BF16),
                      norm2[l][None, :], w_gate[l].astype(BF16), w_up[l].astype(BF16),
                      w_down[l].astype(BF16), final_norm[None, :], b, s,
                      final_norm=(l == depth - 1))
    return x2.reshape(b, s, d)
```
